```python
import math
import jax, jax.numpy as jnp
from jax import lax
import numpy as np

D_MODEL = 2048
BATCH = 4
SEQ = 2048
DEPTH = 2
DEC_BATCH = 128
DEC_SEQ = 4
PAST_LEN = 16384
PAGE_SIZE = 128

GLA_H = 4
GLA_DK = D_MODEL // 16
GLA_DV = D_MODEL // 8
GLA_RANK = 16
GATE_TAU = 16.0
GLA_CHUNK = 64
RET_H = 8
RET_DK = D_MODEL // 32
RET_DV = D_MODEL // 16
RET_CHUNK = 64
ROPE_BASE = 10000.0
CONV_DIM = D_MODEL // 2
CONV_W = 3
D_FF = 4 * D_MODEL
EPS = 1e-6
GN_EPS = 1e-5

IN_SIZES = (
    GLA_H * GLA_DK, GLA_H * GLA_DK, GLA_H * GLA_DV, GLA_RANK, GLA_H * GLA_DV,
    RET_H * RET_DK, RET_H * RET_DK, RET_H * RET_DV, RET_H * RET_DV,
    CONV_DIM, CONV_DIM, CONV_DIM,
    3 * D_MODEL,
)
IN_COLS = sum(IN_SIZES)

kernel_name = "hybrid_gla_retnet_shortconv_decode_step"


def rmsnorm(x, g):
    xf = x.astype(jnp.float32)
    y = xf * lax.rsqrt(jnp.mean(xf * xf, axis=-1, keepdims=True) + EPS)
    return (y * g.astype(jnp.float32)).astype(x.dtype)


def head_rmsnorm(o, g):
    B, T, H, DV = o.shape
    y = o * lax.rsqrt(jnp.mean(o * o, axis=-1, keepdims=True) + EPS)
    y = y * g.astype(jnp.float32).reshape(H, DV)
    return y.reshape(B, T, H * DV)


def head_groupnorm(o, w, b):
    B, T, H, DV = o.shape
    mu = jnp.mean(o, axis=-1, keepdims=True)
    c = o - mu
    y = c * lax.rsqrt(jnp.mean(c * c, axis=-1, keepdims=True) + GN_EPS)
    y = y * w.astype(jnp.float32).reshape(H, DV) + b.astype(jnp.float32).reshape(H, DV)
    return y.reshape(B, T, H * DV)


def rotary(x, pos):
    half = x.shape[-1] // 2
    inv_freq = ROPE_BASE ** (-jnp.arange(half, dtype=jnp.float32) / half)
    ang = pos[:, None] * inv_freq[None, :]
    cos = jnp.cos(ang)[None, :, None, :]
    sin = jnp.sin(ang)[None, :, None, :]
    x1, x2 = x[..., :half], x[..., half:]
    return jnp.concatenate([x1 * cos - x2 * sin, x1 * sin + x2 * cos], axis=-1)


def to_chunks(a, L):
    B, T = a.shape[0], a.shape[1]
    return a.reshape((B, T // L, L) + a.shape[2:]).swapaxes(0, 1)


def from_chunks(a):
    a = a.swapaxes(0, 1)
    return a.reshape((a.shape[0], a.shape[1] * a.shape[2]) + a.shape[3:])


def gla_chunked(q, k, v, log_a, S0):
    T = q.shape[1]
    L = math.gcd(T, GLA_CHUNK)
    causal = jnp.arange(L)[:, None] >= jnp.arange(L)[None, :]
    cmask = causal[None, :, :, None, None]

    def step(S, inp):
        qc, kc, vc, gc = inp
        b = jnp.cumsum(gc, axis=1)
        o_inter = jnp.einsum('blhk,bhkv->blhv', qc * jnp.exp(b), S)
        diff = b[:, :, None] - b[:, None, :]
        decay = jnp.where(cmask, jnp.exp(jnp.where(cmask, diff, 0.0)), 0.0)
        A = jnp.einsum('bthk,bshk,btshk->bths', qc, kc, decay)
        o_intra = jnp.einsum('bths,bshv->bthv', A, vc)
        bL = b[:, -1]
        k_dec = kc * jnp.exp(bL[:, None] - b)
        S_new = jnp.exp(bL)[..., None] * S + jnp.einsum('bshk,bshv->bhkv', k_dec, vc)
        return S_new, o_inter + o_intra

    S_fin, o = lax.scan(step, S0, (to_chunks(q, L), to_chunks(k, L), to_chunks(v, L), to_chunks(log_a, L)))
    return from_chunks(o), S_fin


def retention_chunked(q, k, v, log_gamma, R0):
    T = q.shape[1]
    L = math.gcd(T, RET_CHUNK)
    idx = jnp.arange(L, dtype=jnp.float32)
    diff = idx[:, None] - idx[None, :]
    causal = diff >= 0
    Dm = jnp.where(causal[None], jnp.exp(jnp.where(causal, diff, 0.0)[None] * log_gamma[:, None, None]), 0.0)
    xi = jnp.exp((idx[:, None] + 1.0) * log_gamma[None, :])
    zeta = jnp.exp((L - 1.0 - idx[:, None]) * log_gamma[None, :])
    gL = jnp.exp(L * log_gamma)

    def step(R, inp):
        qc, kc, vc = inp
        o_inter = jnp.einsum('blhk,bhkv->blhv', qc, R) * xi[None, :, :, None]
        A = jnp.einsum('bthk,bshk->bhts', qc, kc) * Dm[None]
        o_intra = jnp.einsum('bhts,bshv->bthv', A, vc)
        R_new = gL[None, :, None, None] * R + jnp.einsum('bshk,bshv->bhkv', kc * zeta[None, :, :, None], vc)
        return R_new, o_inter + o_intra

    R_fin, o = lax.scan(step, R0, (to_chunks(q, L), to_chunks(k, L), to_chunks(v, L)))
    return from_chunks(o), R_fin


def mixer_block(h, s_gla, s_ret, s_conv, pos, w_in, w_gate2, b_gate2, gla_g, ret_w, ret_b, conv_w,
                w_br_gla, w_br_ret, w_br_conv, w_o):
    Bt, T, _ = h.shape
    f32 = jnp.float32
    splits = np.cumsum(np.array(IN_SIZES))[:-1].tolist()
    z = h @ w_in
    qa, ka, va, alr, ra, qr, kr, vr, gr, cb, cc, ch, gates = jnp.split(z, splits, axis=-1)

    qa = qa.astype(f32).reshape(Bt, T, GLA_H, GLA_DK) * (GLA_DK ** -0.5)
    ka = ka.astype(f32).reshape(Bt, T, GLA_H, GLA_DK)
    va = va.astype(f32).reshape(Bt, T, GLA_H, GLA_DV)
    log_a = jax.nn.log_sigmoid((alr @ w_gate2 + b_gate2).astype(f32)) / GATE_TAU
    log_a = log_a.reshape(Bt, T, GLA_H, GLA_DK)
    oa, s_gla_new = gla_chunked(qa, ka, va, log_a, s_gla.astype(f32))
    oa = (head_rmsnorm(oa, gla_g) * jax.nn.silu(ra.astype(f32))).astype(h.dtype)

    log_gamma = jnp.log1p(-jnp.exp2(-5.0 - jnp.arange(RET_H, dtype=f32)))
    qr = rotary(qr.astype(f32).reshape(Bt, T, RET_H, RET_DK), pos) * (RET_DK ** -0.5)
    kr = rotary(kr.astype(f32).reshape(Bt, T, RET_H, RET_DK), pos)
    vr = vr.astype(f32).reshape(Bt, T, RET_H, RET_DV)
    ob, s_ret_new = retention_chunked(qr, kr, vr, log_gamma, s_ret.astype(f32))
    ob = (head_groupnorm(ob, ret_w, ret_b) * jax.nn.silu(gr.astype(f32))).astype(h.dtype)

    u = cc * ch
    ext = jnp.concatenate([s_conv.astype(u.dtype), u], axis=1)
    conv = sum(conv_w[j] * ext[:, j:j + T] for j in range(CONV_W))
    oc = cb * conv
    s_conv_new = ext[:, T:]

    g = jax.nn.sigmoid(gates.astype(f32)).astype(h.dtype)
    ga, gb, gc = jnp.split(g, 3, axis=-1)
    m = ga * (oa @ w_br_gla) + gb * (ob @ w_br_ret) + gc * (oc @ w_br_conv)
    out = m @ w_o
    return out, s_gla_new.astype(s_gla.dtype), s_ret_new.astype(s_ret.dtype), s_conv_new.astype(s_conv.dtype)


def trunk(x, st_gla, st_ret, st_conv, pos, params):
    (norm_mix, w_in, w_gate2, b_gate2, gla_norm, ret_norm_w, ret_norm_b, conv_w,
     w_branch_gla, w_branch_ret, w_branch_conv, w_out, norm_mlp, w_up, w_down, norm_final) = params
    new_gla, new_ret, new_conv = [], [], []
    for l in range(DEPTH):
        h = rmsnorm(x, norm_mix[l])
        mix, sg, sr, sc = mixer_block(h, st_gla[l], st_ret[l], st_conv[l], pos, w_in[l], w_gate2[l], b_gate2[l],
                                      gla_norm[l], ret_norm_w[l], ret_norm_b[l], conv_w[l],
                                      w_branch_gla[l], w_branch_ret[l], w_branch_conv[l], w_out[l])
        x = x + mix
        h = rmsnorm(x, norm_mlp[l])
        x = x + jnp.square(jax.nn.relu(h @ w_up[l])) @ w_down[l]
        new_gla.append(sg)
        new_ret.append(sr)
        new_conv.append(sc)
    y = rmsnorm(x, norm_final)
    return y, jnp.stack(new_gla), jnp.stack(new_ret), jnp.stack(new_conv)


def setup_inputs(seed: int = 0) -> dict:
    key = jax.random.key(seed)
    ks = jax.random.split(key, 24)
    nrm = lambda k, shape, s: jax.random.normal(k, shape, jnp.float32) * s
    D = D_MODEL
    return {
        "x_prompt": nrm(ks[0], (BATCH, SEQ, D), 1.0),
        "x_sample": nrm(ks[1], (DEC_BATCH, DEC_SEQ, D), 1.0),
        "state_gla": nrm(ks[2], (DEPTH, DEC_BATCH, GLA_H, GLA_DK, GLA_DV), 0.5),
        "state_ret": nrm(ks[3], (DEPTH, DEC_BATCH, RET_H, RET_DK, RET_DV), 0.5),
        "state_conv": nrm(ks[4], (DEPTH, DEC_BATCH, CONV_W - 1, CONV_DIM), 0.5),
        "norm_mix": 1.0 + nrm(ks[5], (DEPTH, D), 0.02),
        "w_in": nrm(ks[6], (DEPTH, D, IN_COLS), D ** -0.5),
        "w_gate2": nrm(ks[7], (DEPTH, GLA_RANK, GLA_H * GLA_DK), GLA_RANK ** -0.5),
        "b_gate2": nrm(ks[8], (DEPTH, GLA_H * GLA_DK), 0.1),
        "gla_norm": 1.0 + nrm(ks[9], (DEPTH, GLA_H * GLA_DV), 0.02),
        "ret_norm_w": 1.0 + nrm(ks[10], (DEPTH, RET_H * RET_DV), 0.02),
        "ret_norm_b": nrm(ks[11], (DEPTH, RET_H * RET_DV), 0.02),
        "conv_w": nrm(ks[12], (DEPTH, CONV_W, CONV_DIM), CONV_W ** -0.5),
        "w_branch_gla": nrm(ks[13], (DEPTH, GLA_H * GLA_DV, D), (GLA_H * GLA_DV) ** -0.5),
        "w_branch_ret": nrm(ks[14], (DEPTH, RET_H * RET_DV, D), (RET_H * RET_DV) ** -0.5),
        "w_branch_conv": nrm(ks[15], (DEPTH, CONV_DIM, D), CONV_DIM ** -0.5),
        "w_out": nrm(ks[16], (DEPTH, D, D), D ** -0.5),
        "norm_mlp": 1.0 + nrm(ks[17], (DEPTH, D), 0.02),
        "w_up": nrm(ks[18], (DEPTH, D, D_FF), D ** -0.5),
        "w_down": nrm(ks[19], (DEPTH, D_FF, D), D_FF ** -0.5),
        "norm_final": 1.0 + nrm(ks[20], (D,), 0.02),
    }


def reference(x_prompt, x_sample, state_gla, state_ret, state_conv, norm_mix, w_in, w_gate2, b_gate2,
              gla_norm, ret_norm_w, ret_norm_b, conv_w, w_branch_gla, w_branch_ret, w_branch_conv, w_out,
              norm_mlp, w_up, w_down, norm_final):
    params = (norm_mix, w_in, w_gate2, b_gate2, gla_norm, ret_norm_w, ret_norm_b, conv_w,
              w_branch_gla, w_branch_ret, w_branch_conv, w_out, norm_mlp, w_up, w_down, norm_final)
    Bp, Tp = x_prompt.shape[0], x_prompt.shape[1]
    Ts = x_sample.shape[1]
    z_gla = jnp.zeros((DEPTH, Bp) + state_gla.shape[2:], state_gla.dtype)
    z_ret = jnp.zeros((DEPTH, Bp) + state_ret.shape[2:], state_ret.dtype)
    z_conv = jnp.zeros((DEPTH, Bp) + state_conv.shape[2:], state_conv.dtype)
    pos_p = jnp.arange(Tp, dtype=jnp.float32)
    y_prompt, gla_p, ret_p, conv_p = trunk(x_prompt, z_gla, z_ret, z_conv, pos_p, params)
    pos_s = PAST_LEN + jnp.arange(Ts, dtype=jnp.float32)
    y_sample, gla_s, ret_s, conv_s = trunk(x_sample, state_gla, state_ret, state_conv, pos_s, params)
    return (y_prompt, y_sample, gla_p, ret_p, conv_p, gla_s, ret_s, conv_s)
```

```python
import functools
import math

import numpy as np
import jax
import jax.numpy as jnp
from jax import lax
from jax.experimental import pallas as pl
from jax.experimental.pallas import tpu as pltpu

F32 = jnp.float32
BF16 = jnp.bfloat16

GLA_H = 4
GLA_RANK = 16
GATE_TAU = 16.0
RET_H = 8
ROPE_BASE = 10000.0
CONV_W = 3
EPS = 1e-6
GN_EPS = 1e-5
PAST_LEN = 16384

LANE = 128
VMEM_LIMIT = 60 * 1024 * 1024
TM = 1088
TN = 512
MLP_SLABS = 4


def _dot(a, b):
    return jnp.dot(a.astype(BF16), b.astype(BF16), preferred_element_type=F32)


def _dot_nt(a, b):
    return lax.dot_general(a.astype(BF16), b.astype(BF16), (((1,), (1,)), ((), ())),
                           preferred_element_type=F32)


def _dot_exact_lhs01(m01, x):
    hi = x.astype(BF16)
    r1 = x - hi.astype(F32)
    mid = r1.astype(BF16)
    lo = (r1 - mid.astype(F32)).astype(BF16)
    out = jnp.dot(m01, hi, preferred_element_type=F32)
    out = out + jnp.dot(m01, mid, preferred_element_type=F32)
    return out + jnp.dot(m01, lo, preferred_element_type=F32)


def _rmsnorm(x, g):
    y = x * lax.rsqrt(jnp.mean(x * x, axis=-1, keepdims=True) + EPS)
    return y * g


def _silu(x):
    return x * jax.nn.sigmoid(x)


def _params(sem):
    return pltpu.CompilerParams(dimension_semantics=sem, vmem_limit_bytes=VMEM_LIMIT)


def _inproj_kernel(x_ref, g_ref, wmain_ref, wext_ref, wrank_ref, wg2_ref, bg2_ref,
                   z_ref, loga_ref, h_scr, *, n_plain, shift):
    j = pl.program_id(1)

    @pl.when(j == 0)
    def _():
        h = _rmsnorm(x_ref[...], g_ref[...]).astype(BF16)
        h_scr[...] = h
        alr = jnp.dot(h, wrank_ref[...].astype(BF16), preferred_element_type=F32)
        pre = _dot(alr, wg2_ref[...]) + bg2_ref[...]
        loga_ref[...] = (jnp.minimum(pre, 0.0) - jnp.log1p(jnp.exp(-jnp.abs(pre)))) * (1.0 / GATE_TAU)

    h = h_scr[...]

    @pl.when(j < n_plain)
    def _():
        z_ref[...] = jnp.dot(h, wmain_ref[...].astype(BF16), preferred_element_type=F32)

    @pl.when(j >= n_plain)
    def _():
        tn = wmain_ref.shape[1]
        w = jnp.concatenate([wmain_ref[...].astype(BF16), wext_ref[...].astype(BF16)], axis=1)
        z_ref[...] = jnp.dot(h, w[:, shift:shift + tn], preferred_element_type=F32)


def _inproj(x, norm_g, w_in, w_gate2_pad, b_gate2, layer, n_cols, rank_col):
    m, d = x.shape
    gk = w_gate2_pad.shape[2]
    n_tiles = n_cols // TN
    n_plain = rank_col // TN
    ext_per_tile = TN // LANE
    kern = functools.partial(_inproj_kernel, n_plain=n_plain, shift=GLA_RANK)
    return pl.pallas_call(
        kern,
        grid=(m // TM, n_tiles),
        in_specs=[
            pl.BlockSpec((TM, d), lambda i, j: (i, 0)),
            pl.BlockSpec((None, 1, d), lambda i, j: (layer, 0, 0)),
            pl.BlockSpec((None, d, TN), lambda i, j: (layer, 0, j)),
            pl.BlockSpec((None, d, LANE), lambda i, j: (layer, 0, ext_per_tile * (j + 1))),
            pl.BlockSpec((None, d, LANE), lambda i, j: (layer, 0, rank_col // LANE)),
            pl.BlockSpec((None, LANE, gk), lambda i, j: (layer, 0, 0)),
            pl.BlockSpec((None, 1, gk), lambda i, j: (layer, 0, 0)),
        ],
        out_specs=[
            pl.BlockSpec((TM, TN), lambda i, j: (i, j)),
            pl.BlockSpec((TM, gk), lambda i, j: (i, 0)),
        ],
        out_shape=[jax.ShapeDtypeStruct((m, n_cols), F32),
                   jax.ShapeDtypeStruct((m, gk), F32)],
        scratch_shapes=[pltpu.VMEM((TM, d), BF16)],
        compiler_params=_params(("parallel", "arbitrary")),
        name="inproj",
    )(x, norm_g, w_in, w_in, w_in, w_gate2_pad, b_gate2)


def _gla_levels(seq_len):
    return [m for m in (64, 32, 16, 8, 4, 2, 1) if m < seq_len]


def _gla_scan_matrices(seq_len):
    r = np.arange(LANE)
    mats = []
    for m in [seq_len] + [m for m in _gla_levels(seq_len) if m > 1]:
        same = (r[:, None] // m) == (r[None, :] // m)
        mats.append(same & (r[None, :] <= r[:, None]))
        mats.append(same & (r[None, :] > r[:, None]))
    return jnp.asarray(np.concatenate(mats, axis=0).astype(np.float32), dtype=BF16)


def _gla_tile(q, k, v, g, scan_ref, seq_len, scale):
    levels = _gla_levels(seq_len)
    e = _dot_exact_lhs01(scan_ref[...], g)
    blk = lambda i: e[i * LANE:(i + 1) * LANE]
    qe = q * scale
    c_seq, d_seq = blk(0), blk(1)
    q_inter = qe * jnp.exp(c_seq)
    k_dec = k * jnp.exp(d_seq)

    t = lax.broadcasted_iota(jnp.int32, (LANE, LANE), 0)
    s = lax.broadcasted_iota(jnp.int32, (LANE, LANE), 1)
    a = jnp.where(t == s, _dot_nt(qe, k), 0.0)
    for i, m in enumerate(levels):
        if m > 1:
            qm = qe * jnp.exp(blk(2 + 2 * i))
            km = k * jnp.exp(blk(3 + 2 * i))
        else:
            qm = qe * jnp.exp(g)
            km = k
        sh = int(math.log2(2 * m))
        mask = ((t >> sh) == (s >> sh)) & ((t & m) != 0) & ((s & m) == 0)
        a = jnp.where(mask, _dot_nt(qm, km), a)
    return _dot(a, v), q_inter, k_dec, c_seq + d_seq


def _gla_finish(o, ra, gn):
    y = o * lax.rsqrt(jnp.mean(o * o, axis=-1, keepdims=True) + EPS)
    return ((y * gn) * _silu(ra)).astype(BF16)


def _gla_prompt_kernel(q_ref, k_ref, v_ref, ra_ref, g_ref, gn_ref, scan_ref,
                       o_ref, s_ref, st_scr, *, scale, n_chunks):
    tb = pl.program_id(2)

    @pl.when(tb == 0)
    def _():
        st_scr[...] = jnp.zeros_like(st_scr)

    for c in range(n_chunks):
        rows = pl.ds(c * LANE, LANE)
        v = v_ref[rows, :]
        o_intra, q_inter, k_dec, tot = _gla_tile(q_ref[rows, :], k_ref[rows, :], v, g_ref[rows, :],
                                                 scan_ref, LANE, scale)
        st = st_scr[...]
        o = o_intra + _dot_nt(q_inter, st)
        st_scr[...] = st * jnp.exp(tot[0:1, :]) + _dot(v.T, k_dec)
        o_ref[rows, :] = _gla_finish(o, ra_ref[rows, :], gn_ref[...])

    @pl.when(tb == pl.num_programs(2) - 1)
    def _():
        s_ref[...] = st_scr[...].T


def _gla_prompt(z, loga, gla_norm, layer, n_seq, seq_len, dk, dv, tb):
    m = z.shape[0]
    nt = seq_len // tb
    kern = functools.partial(_gla_prompt_kernel, scale=dk ** -0.5, n_chunks=tb // LANE)
    row = lambda b, h, t: b * nt + t
    v_blk0 = (2 * GLA_H * dk) // dv
    ra_blk0 = (2 * GLA_H * dk + GLA_H * dv) // dv
    scan = _gla_scan_matrices(LANE)
    return pl.pallas_call(
        kern,
        grid=(n_seq, GLA_H, nt),
        in_specs=[
            pl.BlockSpec((tb, dk), lambda b, h, t: (row(b, h, t), h)),
            pl.BlockSpec((tb, dk), lambda b, h, t: (row(b, h, t), GLA_H + h)),
            pl.BlockSpec((tb, dv), lambda b, h, t: (row(b, h, t), v_blk0 + h)),
            pl.BlockSpec((tb, dv), lambda b, h, t: (row(b, h, t), ra_blk0 + h)),
            pl.BlockSpec((tb, dk), lambda b, h, t: (row(b, h, t), h)),
            pl.BlockSpec((None, 1, dv), lambda b, h, t: (layer, 0, h)),
            pl.BlockSpec(scan.shape, lambda b, h, t: (0, 0)),
        ],
        out_specs=[
            pl.BlockSpec((tb, dv), lambda b, h, t: (row(b, h, t), h)),
            pl.BlockSpec((None, None, dk, dv), lambda b, h, t: (b, h, 0, 0)),
        ],
        out_shape=[jax.ShapeDtypeStruct((n_seq * seq_len, GLA_H * dv), BF16),
                   jax.ShapeDtypeStruct((n_seq, GLA_H, dk, dv), F32)],
        scratch_shapes=[pltpu.VMEM((dv, dk), F32)],
        compiler_params=_params(("parallel", "parallel", "arbitrary")),
        name="gla_prompt",
    )(z, z, z, z, loga, gla_norm, scan)


def _gla_sample_kernel(q_ref, k_ref, v_ref, ra_ref, g_ref, gn_ref, scan_ref, s_in_ref,
                       o_ref, s_out_ref, *, scale, seq_len):
    v = v_ref[...]
    o_intra, q_inter, k_dec, tot = _gla_tile(q_ref[...], k_ref[...], v, g_ref[...],
                                             scan_ref, seq_len, scale)
    kd_t = k_dec.T
    dec_t = jnp.exp(tot).T
    q_b, kd_b, v_b = q_inter.astype(BF16), kd_t.astype(BF16), v.astype(BF16)
    row = lax.broadcasted_iota(jnp.int32, (LANE, LANE), 0)
    col = lax.broadcasted_iota(jnp.int32, (LANE, LANE), 1)
    sh = int(math.log2(seq_len))
    n_seq = LANE // seq_len

    def body(b, o_acc):
        s_b = s_in_ref[b]
        o_acc = o_acc + jnp.dot(jnp.where((row >> sh) == b, q_b, jnp.zeros_like(q_b)), s_b.astype(BF16),
                                preferred_element_type=F32)
        upd = jnp.dot(jnp.where((col >> sh) == b, kd_b, jnp.zeros_like(kd_b)), v_b,
                      preferred_element_type=F32)
        dcol = jnp.sum(jnp.where(col == b * seq_len, dec_t, 0.0), axis=1, keepdims=True)
        s_out_ref[b] = dcol * s_b + upd
        return o_acc

    o_inter = lax.fori_loop(0, n_seq, body, jnp.zeros(o_intra.shape, F32))
    o_ref[...] = _gla_finish(o_intra + o_inter, ra_ref[...], gn_ref[...])


def _gla_sample(z, loga, gla_norm, state, layer, row0, n_seq, seq_len, dk, dv):
    per_tile = LANE // seq_len
    n_tiles = n_seq // per_tile
    r0 = row0 // LANE
    kern = functools.partial(_gla_sample_kernel, scale=dk ** -0.5, seq_len=seq_len)
    v_blk0 = (2 * GLA_H * dk) // dv
    ra_blk0 = (2 * GLA_H * dk + GLA_H * dv) // dv
    scan = _gla_scan_matrices(seq_len)
    return pl.pallas_call(
        kern,
        grid=(n_tiles, GLA_H),
        in_specs=[
            pl.BlockSpec((LANE, dk), lambda i, h: (r0 + i, h)),
            pl.BlockSpec((LANE, dk), lambda i, h: (r0 + i, GLA_H + h)),
            pl.BlockSpec((LANE, dv), lambda i, h: (r0 + i, v_blk0 + h)),
            pl.BlockSpec((LANE, dv), lambda i, h: (r0 + i, ra_blk0 + h)),
            pl.BlockSpec((LANE, dk), lambda i, h: (r0 + i, h)),
            pl.BlockSpec((None, 1, dv), lambda i, h: (layer, 0, h)),
            pl.BlockSpec(scan.shape, lambda i, h: (0, 0)),
            pl.BlockSpec((None, per_tile, None, dk, dv), lambda i, h: (layer, i, h, 0, 0)),
        ],
        out_specs=[
            pl.BlockSpec((LANE, dv), lambda i, h: (i, h)),
            pl.BlockSpec((per_tile, None, dk, dv), lambda i, h: (i, h, 0, 0)),
        ],
        out_shape=[jax.ShapeDtypeStruct((n_seq * seq_len, GLA_H * dv), BF16),
                   jax.ShapeDtypeStruct((n_seq, GLA_H, dk, dv), F32)],
        compiler_params=_params(("parallel", "parallel")),
        name="gla_sample",
    )(z, z, z, z, loga, gla_norm, scan, state)


def _ret_tables(seq_len, pos0, n_rows):
    dk = LANE // 2
    half = dk // 2
    log_gamma = jnp.log1p(-jnp.exp2(-5.0 - jnp.arange(RET_H, dtype=F32)))
    pos = (pos0 + (jnp.arange(n_rows) % seq_len)).astype(F32)
    inv_freq = ROPE_BASE ** (-jnp.arange(half, dtype=F32) / half)
    ang = pos[:, None] * inv_freq[None, :]
    cos, sin = jnp.cos(ang), jnp.sin(ang)
    cos_t = jnp.tile(cos, (1, 4))
    sin_t = jnp.tile(jnp.concatenate([-sin, sin], axis=1), (1, 2))
    r = jnp.arange(LANE)
    idx = (r % seq_len).astype(F32)
    diff = idx[:, None] - idx[None, :]
    causal = (diff >= 0) & ((r[:, None] // seq_len) == (r[None, :] // seq_len))
    dm = jnp.where(causal[None], jnp.exp(jnp.where(causal, diff, 0.0)[None] * log_gamma[:, None, None]), 0.0)
    xi = jnp.exp((idx[:, None] + 1.0) * log_gamma[None, :])
    zeta = jnp.exp((seq_len - 1.0 - idx[:, None]) * log_gamma[None, :])
    g_l = jnp.exp(seq_len * log_gamma)
    xi_t = jnp.broadcast_to(xi.T[:, :, None], (RET_H, LANE, LANE))
    zeta_t = jnp.repeat(zeta.reshape(LANE, RET_H // 2, 2), dk, axis=2).transpose(1, 0, 2)
    gl_t = jnp.broadcast_to(jnp.repeat(g_l.reshape(RET_H // 2, 2), dk, axis=1)[:, :, None],
                            (RET_H // 2, LANE, LANE))
    return cos_t, sin_t, dm, xi_t, zeta_t, gl_t


def _ret_tile(q, k, v, cos, sin, dm_ref, scale):
    lane = lax.broadcasted_iota(jnp.int32, q.shape, 1)
    first = (lane & (LANE // 4)) == 0

    def rot(x):
        sw = jnp.where(first, pltpu.roll(x, LANE - LANE // 4, axis=1), pltpu.roll(x, LANE // 4, axis=1))
        return x * cos + sw * sin

    q = rot(q) * scale
    k = rot(k)
    lo = lane < LANE // 2
    q0 = jnp.where(lo, q, 0.0)
    q1 = jnp.where(lo, 0.0, q)
    a0 = _dot_nt(q0, k) * dm_ref[0]
    a1 = _dot_nt(q1, k) * dm_ref[1]
    o0 = _dot(a0, v[:, :LANE])
    o1 = _dot(a1, v[:, LANE:])
    return k, q0, q1, o0, o1


def _ret_finish(o, gr, w, b):
    mu = jnp.mean(o, axis=-1, keepdims=True)
    c = o - mu
    y = c * lax.rsqrt(jnp.mean(c * c, axis=-1, keepdims=True) + GN_EPS)
    return ((y * w + b) * _silu(gr)).astype(BF16)


def _ret_prompt_kernel(q_ref, k_ref, v_ref, gr_ref, cos_ref, sin_ref, dm_ref, xi_ref, zeta_ref, gl_ref,
                       w_ref, b_ref, o_ref, s_ref, p_scr, *, scale, n_chunks):
    tb = pl.program_id(2)

    @pl.when(tb == 0)
    def _():
        p_scr[...] = jnp.zeros_like(p_scr)

    for c in range(n_chunks):
        rows = pl.ds(c * LANE, LANE)
        v = v_ref[rows, :]
        k, q0, q1, o0, o1 = _ret_tile(q_ref[rows, :], k_ref[rows, :], v, cos_ref[rows, :], sin_ref[rows, :],
                                      dm_ref, scale)
        p = p_scr[...]
        o0 = o0 + _dot(q0, p[:, :LANE]) * xi_ref[0]
        o1 = o1 + _dot(q1, p[:, LANE:]) * xi_ref[1]
        gl = gl_ref[...]
        p_scr[...] = jnp.concatenate([gl, gl], axis=1) * p + _dot((k * zeta_ref[...]).T, v)
        gr = gr_ref[rows, :]
        o_ref[rows, :LANE] = _ret_finish(o0, gr[:, :LANE], w_ref[:, :LANE], b_ref[:, :LANE])
        o_ref[rows, LANE:] = _ret_finish(o1, gr[:, LANE:], w_ref[:, LANE:], b_ref[:, LANE:])

    @pl.when(tb == pl.num_programs(2) - 1)
    def _():
        half = LANE // 2
        s_ref[0] = p_scr[:half, :LANE]
        s_ref[1] = p_scr[half:, LANE:]


def _ret_prompt(z, ret_w, ret_b, layer, n_seq, seq_len, col0, tb):
    nt = seq_len // tb
    pairs = RET_H // 2
    dk2, dv2 = LANE, 2 * LANE
    cos_t, sin_t, dm, xi_t, zeta_t, gl_t = _ret_tables(LANE, 0, seq_len)
    cos_t, sin_t = _ret_tables(seq_len, 0, seq_len)[:2]
    kern = functools.partial(_ret_prompt_kernel, scale=(LANE // 2) ** -0.5, n_chunks=tb // LANE)
    row = lambda b, j, t: b * nt + t
    q0 = col0 // dk2
    k0 = (col0 + RET_H * LANE // 2) // dk2
    v0 = (col0 + RET_H * LANE) // dv2
    g0 = (col0 + 2 * RET_H * LANE) // dv2
    return pl.pallas_call(
        kern,
        grid=(n_seq, pairs, nt),
        in_specs=[
            pl.BlockSpec((tb, dk2), lambda b, j, t: (row(b, j, t), q0 + j)),
            pl.BlockSpec((tb, dk2), lambda b, j, t: (row(b, j, t), k0 + j)),
            pl.BlockSpec((tb, dv2), lambda b, j, t: (row(b, j, t), v0 + j)),
            pl.BlockSpec((tb, dv2), lambda b, j, t: (row(b, j, t), g0 + j)),
            pl.BlockSpec((tb, LANE), lambda b, j, t: (t, 0)),
            pl.BlockSpec((tb, LANE), lambda b, j, t: (t, 0)),
            pl.BlockSpec((2, LANE, LANE), lambda b, j, t: (j, 0, 0)),
            pl.BlockSpec((2, LANE, LANE), lambda b, j, t: (j, 0, 0)),
            pl.BlockSpec((None, LANE, LANE), lambda b, j, t: (j, 0, 0)),
            pl.BlockSpec((None, LANE, LANE), lambda b, j, t: (j, 0, 0)),
            pl.BlockSpec((None, 1, dv2), lambda b, j, t: (layer, 0, j)),
            pl.BlockSpec((None, 1, dv2), lambda b, j, t: (layer, 0, j)),
        ],
        out_specs=[
            pl.BlockSpec((tb, dv2), lambda b, j, t: (row(b, j, t), j)),
            pl.BlockSpec((None, 2, LANE // 2, LANE), lambda b, j, t: (b, j, 0, 0)),
        ],
        out_shape=[jax.ShapeDtypeStruct((n_seq * seq_len, RET_H * LANE), BF16),
                   jax.ShapeDtypeStruct((n_seq, RET_H, LANE // 2, LANE), F32)],
        scratch_shapes=[pltpu.VMEM((dk2, dv2), F32)],
        compiler_params=_params(("parallel", "parallel", "arbitrary")),
        name="ret_prompt",
    )(z, z, z, z, cos_t, sin_t, dm, xi_t, zeta_t, gl_t, ret_w, ret_b)


def _ret_sample_kernel(q_ref, k_ref, v_ref, gr_ref, cos_ref, sin_ref, dm_ref, xi_ref, zeta_ref, gl_ref,
                       w_ref, b_ref, s_in_ref, o_ref, s_out_ref, *, scale, seq_len):
    v = v_ref[...]
    k, q0, q1, o0, o1 = _ret_tile(q_ref[...], k_ref[...], v, cos_ref[...], sin_ref[...], dm_ref, scale)
    kz_t = (k * zeta_ref[...]).T.astype(BF16)
    q0_b, q1_b, v_b = q0.astype(BF16), q1.astype(BF16), v.astype(BF16)
    gl = gl_ref[...]
    row = lax.broadcasted_iota(jnp.int32, (LANE, LANE), 0)
    col = lax.broadcasted_iota(jnp.int32, (LANE, LANE), 1)
    sh = int(math.log2(seq_len))
    half = LANE // 2

    def body(b, carry):
        acc0, acc1 = carry
        p_b = s_in_ref[b].reshape(LANE, LANE)
        p_bb = p_b.astype(BF16)
        mine = (row >> sh) == b
        acc0 = acc0 + jnp.dot(jnp.where(mine, q0_b, jnp.zeros_like(q0_b)), p_bb, preferred_element_type=F32)
        acc1 = acc1 + jnp.dot(jnp.where(mine, q1_b, jnp.zeros_like(q1_b)), p_bb, preferred_element_type=F32)
        upd = jnp.dot(jnp.where((col >> sh) == b, kz_t, jnp.zeros_like(kz_t)), v_b,
                      preferred_element_type=F32)
        new = gl * p_b + jnp.concatenate([upd[:half, :LANE], upd[half:, LANE:]], axis=0)
        s_out_ref[b] = new.reshape(2, half, LANE)
        return acc0, acc1

    zero = jnp.zeros((LANE, LANE), F32)
    acc0, acc1 = lax.fori_loop(0, LANE // seq_len, body, (zero, zero))
    gr = gr_ref[...]
    o_ref[:, :LANE] = _ret_finish(o0 + acc0 * xi_ref[0], gr[:, :LANE], w_ref[:, :LANE], b_ref[:, :LANE])
    o_ref[:, LANE:] = _ret_finish(o1 + acc1 * xi_ref[1], gr[:, LANE:], w_ref[:, LANE:], b_ref[:, LANE:])


def _ret_sample(z, ret_w, ret_b, state, layer, row0, n_seq, seq_len, col0):
    per_tile = LANE // seq_len
    n_tiles = n_seq // per_tile
    pairs = RET_H // 2
    dk2, dv2 = LANE, 2 * LANE
    r0 = row0 // LANE
    tables = _ret_tables(seq_len, PAST_LEN, LANE)
    kern = functools.partial(_ret_sample_kernel, scale=(LANE // 2) ** -0.5, seq_len=seq_len)
    q0 = col0 // dk2
    k0 = (col0 + RET_H * LANE // 2) // dk2
    v0 = (col0 + RET_H * LANE) // dv2
    g0 = (col0 + 2 * RET_H * LANE) // dv2
    return pl.pallas_call(
        kern,
        grid=(n_tiles, pairs),
        in_specs=[
            pl.BlockSpec((LANE, dk2), lambda i, j: (r0 + i, q0 + j)),
            pl.BlockSpec((LANE, dk2), lambda i, j: (r0 + i, k0 + j)),
            pl.BlockSpec((LANE, dv2), lambda i, j: (r0 + i, v0 + j)),
            pl.BlockSpec((LANE, dv2), lambda i, j: (r0 + i, g0 + j)),
            pl.BlockSpec((LANE, LANE), lambda i, j: (0, 0)),
            pl.BlockSpec((LANE, LANE), lambda i, j: (0, 0)),
            pl.BlockSpec((2, LANE, LANE), lambda i, j: (j, 0, 0)),
            pl.BlockSpec((2, LANE, LANE), lambda i, j: (j, 0, 0)),
            pl.BlockSpec((None, LANE, LANE), lambda i, j: (j, 0, 0)),
            pl.BlockSpec((None, LANE, LANE), lambda i, j: (j, 0, 0)),
            pl.BlockSpec((None, 1, dv2), lambda i, j: (layer, 0, j)),
            pl.BlockSpec((None, 1, dv2), lambda i, j: (layer, 0, j)),
            pl.BlockSpec((None, per_tile, 2, LANE // 2, LANE), lambda i, j: (layer, i, j, 0, 0)),
        ],
        out_specs=[
            pl.BlockSpec((LANE, dv2), lambda i, j: (i, j)),
            pl.BlockSpec((per_tile, 2, LANE // 2, LANE), lambda i, j: (i, j, 0, 0)),
        ],
        out_shape=[jax.ShapeDtypeStruct((n_seq * seq_len, RET_H * LANE), BF16),
                   jax.ShapeDtypeStruct((n_seq, RET_H, LANE // 2, LANE), F32)],
        compiler_params=_params(("parallel", "parallel")),
        name="ret_sample",
    )(z, z, z, z, *tables, ret_w, ret_b, state)


def _conv_kernel(cb_ref, cc_ref, ch_ref, w_ref, p1_ref, p2_ref, o_ref, u_ref, *, seq_len):
    u = cc_ref[...] * ch_ref[...]
    t = lax.broadcasted_iota(jnp.int32, u.shape, 0) & (seq_len - 1)
    um1 = jnp.where(t >= 1, pltpu.roll(u, 1, axis=0), p1_ref[...])
    um2 = jnp.where(t >= 2, pltpu.roll(u, 2, axis=0), p2_ref[...])
    conv = w_ref[0:1, :] * um2 + w_ref[1:2, :] * um1 + w_ref[2:3, :] * u
    o_ref[...] = (cb_ref[...] * conv).astype(BF16)
    u_ref[...] = u


def _conv(z, conv_w, prev1, prev2, layer, row0, n_rows, seq_len, col0, cdim, rows_blk, cols_blk):
    shared = prev1.shape[0] == rows_blk
    kern = functools.partial(_conv_kernel, seq_len=seq_len)
    r0 = row0 // rows_blk
    nb = cdim // cols_blk
    c0 = col0 // cols_blk
    return pl.pallas_call(
        kern,
        grid=(n_rows // rows_blk, nb),
        in_specs=[
            pl.BlockSpec((rows_blk, cols_blk), lambda r, c: (r0 + r, c0 + c)),
            pl.BlockSpec((rows_blk, cols_blk), lambda r, c: (r0 + r, c0 + nb + c)),
            pl.BlockSpec((rows_blk, cols_blk), lambda r, c: (r0 + r, c0 + 2 * nb + c)),
            pl.BlockSpec((None, CONV_W, cols_blk), lambda r, c: (layer, 0, c)),
            pl.BlockSpec((rows_blk, cols_blk), lambda r, c: (0 if shared else r, c)),
            pl.BlockSpec((rows_blk, cols_blk), lambda r, c: (0 if shared else r, c)),
        ],
        out_specs=[
            pl.BlockSpec((rows_blk, cols_blk), lambda r, c: (r, c)),
            pl.BlockSpec((rows_blk, cols_blk), lambda r, c: (r, c)),
        ],
        out_shape=[jax.ShapeDtypeStruct((n_rows, cdim), BF16),
                   jax.ShapeDtypeStruct((n_rows, cdim), F32)],
        compiler_params=_params(("parallel", "parallel")),
        name="conv",
    )(z, z, z, conv_w, prev1, prev2)


def _merge_kernel(oa_ref, ob_ref, oc_ref, ga_ref, gb_ref, gc_ref, wa_ref, wb_ref, wc_ref, m_ref):
    m = jax.nn.sigmoid(ga_ref[...]) * _dot(oa_ref[...], wa_ref[...])
    m = m + jax.nn.sigmoid(gb_ref[...]) * _dot(ob_ref[...], wb_ref[...])
    m = m + jax.nn.sigmoid(gc_ref[...]) * _dot(oc_ref[...], wc_ref[...])
    m_ref[...] = m.astype(BF16)


def _merge(oa, ob, oc, z, wa, wb, wc, layer, gate_col0, d):
    m, kdim = oa.shape
    g0 = gate_col0 // TN
    nd = d // TN
    o_spec = pl.BlockSpec((TM, kdim), lambda i, j: (i, 0))
    w_spec = pl.BlockSpec((None, kdim, TN), lambda i, j: (layer, 0, j))
    return pl.pallas_call(
        _merge_kernel,
        grid=(m // TM, nd),
        in_specs=[o_spec, o_spec, o_spec,
                  pl.BlockSpec((TM, TN), lambda i, j: (i, g0 + j)),
                  pl.BlockSpec((TM, TN), lambda i, j: (i, g0 + nd + j)),
                  pl.BlockSpec((TM, TN), lambda i, j: (i, g0 + 2 * nd + j)),
                  w_spec, w_spec, w_spec],
        out_specs=pl.BlockSpec((TM, TN), lambda i, j: (i, j)),
        out_shape=jax.ShapeDtypeStruct((m, d), BF16),
        compiler_params=_params(("parallel", "arbitrary")),
        name="merge",
    )(oa, ob, oc, z, z, z, wa, wb, wc)


def _outproj_kernel(m_ref, w_ref, x_ref, o_ref):
    o_ref[...] = x_ref[...] + _dot(m_ref[...], w_ref[...])


def _outproj(mm, w_out, x, layer):
    m, d = x.shape
    return pl.pallas_call(
        _outproj_kernel,
        grid=(m // TM, d // TN),
        in_specs=[pl.BlockSpec((TM, d), lambda i, j: (i, 0)),
                  pl.BlockSpec((None, d, TN), lambda i, j: (layer, 0, j)),
                  pl.BlockSpec((TM, TN), lambda i, j: (i, j))],
        out_specs=pl.BlockSpec((TM, TN), lambda i, j: (i, j)),
        out_shape=jax.ShapeDtypeStruct((m, d), F32),
        compiler_params=_params(("parallel", "arbitrary")),
        name="outproj",
    )(mm, w_out, x)


def _mlp_kernel(x_ref, g_ref, wu_ref, wd_ref, o_ref, h_scr):
    f = pl.program_id(1)

    @pl.when(f == 0)
    def _():
        x = x_ref[...]
        h_scr[...] = _rmsnorm(x, g_ref[...]).astype(BF16)
        o_ref[...] = x

    wu = wu_ref[...].astype(BF16)
    wd = wd_ref[...].astype(BF16)
    slab = o_ref.shape[0] // MLP_SLABS
    for r in range(MLP_SLABS):
        rows = pl.ds(r * slab, slab)
        a = jnp.maximum(jnp.dot(h_scr[rows, :], wu, preferred_element_type=F32), 0.0)
        o_ref[rows, :] += jnp.dot((a * a).astype(BF16), wd, preferred_element_type=F32)


def _mlp(x, norm_g, w_up, w_down, layer, tf):
    m, d = x.shape
    dff = w_up.shape[2]
    return pl.pallas_call(
        _mlp_kernel,
        grid=(m // TM, dff // tf),
        in_specs=[pl.BlockSpec((TM, d), lambda i, f: (i, 0), pipeline_mode=pl.Buffered(1)),
                  pl.BlockSpec((None, 1, d), lambda i, f: (layer, 0, 0)),
                  pl.BlockSpec((None, d, tf), lambda i, f: (layer, 0, f)),
                  pl.BlockSpec((None, tf, d), lambda i, f: (layer, f, 0))],
        out_specs=pl.BlockSpec((TM, d), lambda i, f: (i, 0)),
        out_shape=jax.ShapeDtypeStruct((m, d), F32),
        scratch_shapes=[pltpu.VMEM((TM, d), BF16)],
        compiler_params=_params(("parallel", "arbitrary")),
        name="mlp",
    )(x, norm_g, w_up, w_down)


def _final_norm_kernel(x_ref, g_ref, o_ref):
    o_ref[...] = _rmsnorm(x_ref[...], g_ref[...])


def _final_norm(x, g):
    m, d = x.shape
    tm = 512
    return pl.pallas_call(
        _final_norm_kernel,
        grid=(m // tm,),
        in_specs=[pl.BlockSpec((tm, d), lambda i: (i, 0)),
                  pl.BlockSpec((1, d), lambda i: (0, 0))],
        out_specs=pl.BlockSpec((tm, d), lambda i: (i, 0)),
        out_shape=jax.ShapeDtypeStruct((m, d), F32),
        compiler_params=_params(("parallel",)),
        name="final_norm",
    )(x, g)


def kernel(x_prompt, x_sample, state_gla, state_ret, state_conv, norm_mix, w_in, w_gate2, b_gate2, gla_norm, ret_norm_w, ret_norm_b, conv_w, w_branch_gla, w_branch_ret, w_branch_conv, w_out, norm_mlp, w_up, w_down, norm_final):
    bp, tp, d = x_prompt.shape
    bs, ts, _ = x_sample.shape
    depth = w_in.shape[0]
    dk, dv = state_gla.shape[3], state_gla.shape[4]
    cdim = state_conv.shape[3]
    mp, ms = bp * tp, bs * ts
    rank_col = 2 * GLA_H * dk + GLA_H * dv
    n_cols = w_in.shape[2] - GLA_RANK
    ret_col0 = rank_col + GLA_H * dv
    conv_col0 = ret_col0 + 3 * RET_H * LANE
    gate_col0 = conv_col0 + 3 * cdim

    x = jnp.concatenate([x_prompt.reshape(mp, d), x_sample.reshape(ms, d)], axis=0)
    r3 = lambda a: a.reshape(depth, 1, a.shape[-1])
    norm_mix3, norm_mlp3, b_gate23 = r3(norm_mix), r3(norm_mlp), r3(b_gate2)
    gla_norm3, ret_w3, ret_b3 = r3(gla_norm), r3(ret_norm_w), r3(ret_norm_b)
    w_gate2_pad = jnp.pad(w_gate2, ((0, 0), (0, LANE - GLA_RANK), (0, 0)))
    zeros_p = jnp.zeros((tp, cdim), F32)

    new_gla_p, new_ret_p, new_conv_p, new_gla_s, new_ret_s, new_conv_s = [], [], [], [], [], []
    for l in range(depth):
        z, loga = _inproj(x, norm_mix3, w_in, w_gate2_pad, b_gate23, l, n_cols, rank_col)

        oa_p, sg_p = _gla_prompt(z, loga, gla_norm3, l, bp, tp, dk, dv, tb=2 * LANE)
        oa_s, sg_s = _gla_sample(z, loga, gla_norm3, state_gla, l, mp, bs, ts, dk, dv)
        ob_p, sr_p = _ret_prompt(z, ret_w3, ret_b3, l, bp, tp, ret_col0, tb=2 * LANE)
        ob_s, sr_s = _ret_sample(z, ret_w3, ret_b3, state_ret, l, mp, bs, ts, ret_col0)

        oc_p, u_p = _conv(z, conv_w, zeros_p, zeros_p, l, 0, mp, tp, conv_col0, cdim, tp, 2 * LANE)
        sc = state_conv[l]
        pad_rows = jnp.zeros((bs, ts - 1, cdim), F32)
        prev1 = jnp.concatenate([sc[:, 1:2], pad_rows], axis=1).reshape(ms, cdim)
        prev2 = jnp.concatenate([sc, pad_rows[:, 1:]], axis=1).reshape(ms, cdim)
        oc_s, u_s = _conv(z, conv_w, prev1, prev2, l, mp, ms, ts, conv_col0, cdim, ms, 2 * LANE)

        oa = jnp.concatenate([oa_p, oa_s], axis=0)
        ob = jnp.concatenate([ob_p, ob_s], axis=0)
        oc = jnp.concatenate([oc_p, oc_s], axis=0)
        mm = _merge(oa, ob, oc, z, w_branch_gla, w_branch_ret, w_branch_conv, l, gate_col0, d)
        x = _outproj(mm, w_out, x, l)
        x = _mlp(x, norm_mlp3, w_up, w_down, l, tf=TN)

        new_gla_p.append(sg_p)
        new_ret_p.append(sr_p)
        new_conv_p.append(u_p.reshape(bp, tp, cdim)[:, tp - (CONV_W - 1):])
        new_gla_s.append(sg_s)
        new_ret_s.append(sr_s)
        new_conv_s.append(u_s.reshape(bs, ts, cdim)[:, ts - (CONV_W - 1):])

    y = _final_norm(x, norm_final.reshape(1, d))
    return (y[:mp].reshape(bp, tp, d), y[mp:].reshape(bs, ts, d),
            jnp.stack(new_gla_p), jnp.stack(new_ret_p), jnp.stack(new_conv_p),
            jnp.stack(new_gla_s), jnp.stack(new_ret_s), jnp.stack(new_conv_s))
```

```python
import functools
import math

import numpy as np
import jax
import jax.numpy as jnp
from jax import lax
from jax.experimental import pallas as pl
from jax.experimental.pallas import tpu as pltpu

F32 = jnp.float32
BF16 = jnp.bfloat16

GLA_H = 4
GLA_RANK = 16
GATE_TAU = 16.0
RET_H = 8
ROPE_BASE = 10000.0
CONV_W = 3
EPS = 1e-6
GN_EPS = 1e-5
PAST_LEN = 16384

LANE = 128
VMEM_LIMIT = 60 * 1024 * 1024
TM = 2176
TM_MLP = 1088
TN = 512
TN_MERGE = 256
TR = 512
MLP_SLABS = 4
SINGLE = pl.Buffered(1)


def _dot(a, b):
    return jnp.dot(a.astype(BF16), b.astype(BF16), preferred_element_type=F32)


def _dot_nt(a, b):
    return lax.dot_general(a.astype(BF16), b.astype(BF16), (((1,), (1,)), ((), ())),
                           preferred_element_type=F32)


def _rmsnorm(x, g):
    y = x * lax.rsqrt(jnp.mean(x * x, axis=-1, keepdims=True) + EPS)
    return y * g


def _silu(x):
    return x * jax.nn.sigmoid(x)


def _params(sem):
    return pltpu.CompilerParams(dimension_semantics=sem, vmem_limit_bytes=VMEM_LIMIT)


def _prenorm_body(x, g_ref, wrank_ref, wg2_ref, bg2_ref, h_ref, loga_ref):
    h = _rmsnorm(x, g_ref[...]).astype(BF16)
    h_ref[...] = h
    alr = jnp.dot(h, wrank_ref[...].astype(BF16), preferred_element_type=F32)
    pre = _dot(alr, wg2_ref[...]) + bg2_ref[...]
    loga_ref[...] = (jnp.minimum(pre, 0.0) - jnp.log1p(jnp.exp(-jnp.abs(pre)))) * (1.0 / GATE_TAU)


def _prenorm_kernel(x_ref, g_ref, wrank_ref, wg2_ref, bg2_ref, h_ref, loga_ref):
    _prenorm_body(x_ref[...], g_ref, wrank_ref, wg2_ref, bg2_ref, h_ref, loga_ref)


def _prenorm2_kernel(xp_ref, xs_ref, g_ref, wrank_ref, wg2_ref, bg2_ref, h_ref, loga_ref, *, n_first):
    i = pl.program_id(0)

    @pl.when(i < n_first)
    def _():
        _prenorm_body(xp_ref[...], g_ref, wrank_ref, wg2_ref, bg2_ref, h_ref, loga_ref)

    @pl.when(i >= n_first)
    def _():
        _prenorm_body(xs_ref[...], g_ref, wrank_ref, wg2_ref, bg2_ref, h_ref, loga_ref)


def _prenorm(xs, norm_g, w_in, w_gate2_pad, b_gate2, layer, rank_col):
    d = xs[0].shape[1]
    m = sum(x.shape[0] for x in xs)
    gk = w_gate2_pad.shape[2]
    if len(xs) == 1:
        kern = _prenorm_kernel
        x_specs = [pl.BlockSpec((TR, d), lambda i: (i, 0))]
    else:
        n_first = xs[0].shape[0] // TR
        kern = functools.partial(_prenorm2_kernel, n_first=n_first)
        x_specs = [pl.BlockSpec((TR, d), lambda i: (jnp.minimum(i, n_first - 1), 0)),
                   pl.BlockSpec((TR, d), lambda i: (jnp.maximum(i - n_first, 0), 0))]
    return pl.pallas_call(
        kern,
        grid=(m // TR,),
        in_specs=x_specs + [
            pl.BlockSpec((None, 1, d), lambda i: (layer, 0, 0)),
            pl.BlockSpec((None, d, LANE), lambda i: (layer, 0, rank_col // LANE)),
            pl.BlockSpec((None, LANE, gk), lambda i: (layer, 0, 0)),
            pl.BlockSpec((None, 1, gk), lambda i: (layer, 0, 0)),
        ],
        out_specs=[pl.BlockSpec((TR, d), lambda i: (i, 0)),
                   pl.BlockSpec((TR, gk), lambda i: (i, 0))],
        out_shape=[jax.ShapeDtypeStruct((m, d), BF16),
                   jax.ShapeDtypeStruct((m, gk), F32)],
        compiler_params=_params(("parallel",)),
        name="prenorm",
    )(*xs, norm_g, w_in, w_gate2_pad, b_gate2)


def _inproj_kernel(h_ref, wmain_ref, wext_ref, z_ref, *, n_plain, shift):
    j = pl.program_id(1)

    @pl.when(j < n_plain)
    def _():
        z_ref[...] = jnp.dot(h_ref[...], wmain_ref[...].astype(BF16), preferred_element_type=F32)

    @pl.when(j >= n_plain)
    def _():
        tn = wmain_ref.shape[1]
        w = jnp.concatenate([wmain_ref[...].astype(BF16), wext_ref[...].astype(BF16)], axis=1)
        z_ref[...] = jnp.dot(h_ref[...], w[:, shift:shift + tn], preferred_element_type=F32)


def _inproj(h, w_in, layer, n_cols, rank_col):
    m, d = h.shape
    ext_per_tile = TN // LANE
    kern = functools.partial(_inproj_kernel, n_plain=rank_col // TN, shift=GLA_RANK)
    return pl.pallas_call(
        kern,
        grid=(m // TM, n_cols // TN),
        in_specs=[
            pl.BlockSpec((TM, d), lambda i, j: (i, 0)),
            pl.BlockSpec((None, d, TN), lambda i, j: (layer, 0, j)),
            pl.BlockSpec((None, d, LANE), lambda i, j: (layer, 0, ext_per_tile * (j + 1))),
        ],
        out_specs=pl.BlockSpec((TM, TN), lambda i, j: (i, j)),
        out_shape=jax.ShapeDtypeStruct((m, n_cols), F32),
        compiler_params=_params(("parallel", "arbitrary")),
        name="inproj",
    )(h, w_in, w_in)


def _block_sums(g, top):
    row = lax.broadcasted_iota(jnp.int32, g.shape, 0)
    n = g.shape[0]
    c, d, tot = g, jnp.zeros_like(g), g
    out = {1: (c, d)}
    m = 1
    while m < top:
        later = (row & m) != 0
        tot_prev = pltpu.roll(tot, m, axis=0)
        tot_next = pltpu.roll(tot, n - m, axis=0)
        c = c + jnp.where(later, tot_prev, 0.0)
        d = d + jnp.where(later, 0.0, tot_next)
        tot = tot + jnp.where(later, tot_prev, tot_next)
        m *= 2
        out[m] = (c, d)
    return out, tot


def _gla_intra(qe, k, v, sums, seq_len):
    t = lax.broadcasted_iota(jnp.int32, (LANE, LANE), 0)
    s = lax.broadcasted_iota(jnp.int32, (LANE, LANE), 1)
    a = jnp.where(t == s, _dot_nt(qe, k), 0.0)
    m = 1
    while m < seq_len:
        c, d = sums[m]
        km = k if m == 1 else k * jnp.exp(d)
        sh = int(math.log2(2 * m))
        mask = ((t >> sh) == (s >> sh)) & ((t & m) != 0) & ((s & m) == 0)
        a = jnp.where(mask, _dot_nt(qe * jnp.exp(c), km), a)
        m *= 2
    return _dot(a, v)


def _gla_finish(o, ra, gn):
    y = o * lax.rsqrt(jnp.mean(o * o, axis=-1, keepdims=True) + EPS)
    return ((y * gn) * _silu(ra)).astype(BF16)


def _gla_prompt_kernel(q_ref, k_ref, v_ref, ra_ref, g_ref, gn_ref, o_ref, s_ref, st_scr,
                       *, scale, n_chunks, dk, dv):
    tb = pl.program_id(1)

    @pl.when(tb == 0)
    def _():
        st_scr[...] = jnp.zeros_like(st_scr)

    for c in range(n_chunks):
        rows = pl.ds(c * LANE, LANE)
        sums_all, tot_all = _block_sums(g_ref[rows, :], LANE)
        for h in range(GLA_H):
            kl = slice(h * dk, (h + 1) * dk)
            vl = slice(h * dv, (h + 1) * dv)
            sums = {m: (cd[0][:, kl], cd[1][:, kl]) for m, cd in sums_all.items()}
            qe = q_ref[rows, kl] * scale
            k = k_ref[rows, kl]
            v = v_ref[rows, vl]
            c_seq, d_seq = sums[LANE]
            st = st_scr[h]
            o = _gla_intra(qe, k, v, sums, LANE) + _dot_nt(qe * jnp.exp(c_seq), st)
            st_scr[h] = st * jnp.exp(tot_all[0:1, kl]) + _dot(v.T, k * jnp.exp(d_seq))
            o_ref[rows, vl] = _gla_finish(o, ra_ref[rows, vl], gn_ref[:, vl])

    @pl.when(tb == pl.num_programs(1) - 1)
    def _():
        for h in range(GLA_H):
            s_ref[h] = st_scr[h].T


def _gla_prompt(z, loga, gla_norm, layer, n_seq, seq_len, dk, dv, tb):
    m = z.shape[0]
    nt = seq_len // tb
    hk, hv = GLA_H * dk, GLA_H * dv
    kern = functools.partial(_gla_prompt_kernel, scale=dk ** -0.5, n_chunks=tb // LANE, dk=dk, dv=dv)
    row = lambda b, t: b * nt + t
    return pl.pallas_call(
        kern,
        grid=(n_seq, nt),
        in_specs=[
            pl.BlockSpec((tb, hk), lambda b, t: (row(b, t), 0)),
            pl.BlockSpec((tb, hk), lambda b, t: (row(b, t), 1)),
            pl.BlockSpec((tb, hv), lambda b, t: (row(b, t), (2 * hk) // hv)),
            pl.BlockSpec((tb, hv), lambda b, t: (row(b, t), (2 * hk + hv) // hv)),
            pl.BlockSpec((tb, hk), lambda b, t: (row(b, t), 0)),
            pl.BlockSpec((None, 1, hv), lambda b, t: (layer, 0, 0)),
        ],
        out_specs=[
            pl.BlockSpec((tb, hv), lambda b, t: (row(b, t), 0)),
            pl.BlockSpec((None, GLA_H, dk, dv), lambda b, t: (b, 0, 0, 0)),
        ],
        out_shape=[jax.ShapeDtypeStruct((m, hv), BF16),
                   jax.ShapeDtypeStruct((n_seq, GLA_H, dk, dv), F32)],
        scratch_shapes=[pltpu.VMEM((GLA_H, dv, dk), F32)],
        compiler_params=_params(("parallel", "arbitrary")),
        name="gla_prompt",
    )(z, z, z, z, loga, gla_norm)


def _gla_sample_kernel(*refs, scale, seq_len):
    q_ref, k_ref, v_ref, ra_ref, g_ref, gn_ref, s_in_ref = refs[:7]
    o_ref, s_out_ref = refs[-2:]
    n = LANE // seq_len
    sh = int(math.log2(seq_len))
    sums, tot = _block_sums(g_ref[...], seq_len)
    c_seq, d_seq = sums[seq_len]
    qe = q_ref[...] * scale
    k = k_ref[...]
    v = v_ref[...]
    dk, dv = k.shape[1], v.shape[1]
    o = _gla_intra(qe, k, v, sums, seq_len)

    q_b = (qe * jnp.exp(c_seq)).astype(BF16)
    row = lax.broadcasted_iota(jnp.int32, q_b.shape, 0)
    lhs = jnp.concatenate([jnp.where((row >> sh) == b, q_b, jnp.zeros_like(q_b)) for b in range(n)], axis=1)
    s_all = s_in_ref[...]
    o = o + jnp.dot(lhs, s_all.reshape(n * dk, dv).astype(BF16), preferred_element_type=F32)
    o_ref[...] = _gla_finish(o, ra_ref[...], gn_ref[...])

    kd_t = (k * jnp.exp(d_seq)).T.astype(BF16)
    dec_t = jnp.exp(tot).T
    seq = lax.broadcasted_iota(jnp.int32, (n, dk, LANE), 0)
    col = lax.broadcasted_iota(jnp.int32, (n, dk, LANE), 2)
    kd_blk = jnp.where((col >> sh) == seq, kd_t[None], jnp.zeros((), BF16))
    upd = jnp.dot(kd_blk.reshape(n * dk, LANE), v.astype(BF16), preferred_element_type=F32)
    dcol = jnp.sum(jnp.where(col == seq * seq_len, dec_t[None], 0.0), axis=2, keepdims=True)
    s_out_ref[...] = dcol * s_all + upd.reshape(n, dk, dv)


def _gla_sample(z, loga, gla_norm, state, o_full, s_full, layer, row0, n_seq, seq_len, dk, dv):
    depth = state.shape[0]
    per_tile = LANE // seq_len
    r0 = row0 // LANE
    kern = functools.partial(_gla_sample_kernel, scale=dk ** -0.5, seq_len=seq_len)
    v_blk0 = (2 * GLA_H * dk) // dv
    ra_blk0 = (2 * GLA_H * dk + GLA_H * dv) // dv
    carried = [o_full] if s_full is None else [o_full, s_full]
    return pl.pallas_call(
        kern,
        grid=(n_seq // per_tile, GLA_H),
        in_specs=[
            pl.BlockSpec((LANE, dk), lambda i, h: (r0 + i, h)),
            pl.BlockSpec((LANE, dk), lambda i, h: (r0 + i, GLA_H + h)),
            pl.BlockSpec((LANE, dv), lambda i, h: (r0 + i, v_blk0 + h)),
            pl.BlockSpec((LANE, dv), lambda i, h: (r0 + i, ra_blk0 + h)),
            pl.BlockSpec((LANE, dk), lambda i, h: (r0 + i, h)),
            pl.BlockSpec((None, 1, dv), lambda i, h: (layer, 0, h)),
            pl.BlockSpec((None, per_tile, None, dk, dv), lambda i, h: (layer, i, h, 0, 0)),
        ] + [pl.BlockSpec(memory_space=pl.ANY)] * len(carried),
        out_specs=[
            pl.BlockSpec((LANE, dv), lambda i, h: (r0 + i, h)),
            pl.BlockSpec((None, per_tile, None, dk, dv), lambda i, h: (layer, i, h, 0, 0)),
        ],
        out_shape=[jax.ShapeDtypeStruct(o_full.shape, BF16),
                   jax.ShapeDtypeStruct((depth, n_seq, GLA_H, dk, dv), F32)],
        input_output_aliases={7 + n: n for n in range(len(carried))},
        compiler_params=_params(("parallel", "parallel")),
        name="gla_sample",
    )(z, z, z, z, loga, gla_norm, state, *carried)


def _ret_tables(seq_len, pos0, n_rows):
    dk = LANE // 2
    half = dk // 2
    log_gamma = jnp.log1p(-jnp.exp2(-5.0 - jnp.arange(RET_H, dtype=F32)))
    pos = (pos0 + (jnp.arange(n_rows) % seq_len)).astype(F32)
    inv_freq = ROPE_BASE ** (-jnp.arange(half, dtype=F32) / half)
    ang = pos[:, None] * inv_freq[None, :]
    cos, sin = jnp.cos(ang), jnp.sin(ang)
    cos_t = jnp.tile(cos, (1, 4))
    sin_t = jnp.tile(jnp.concatenate([-sin, sin], axis=1), (1, 2))
    r = jnp.arange(LANE)
    idx = (r % seq_len).astype(F32)
    diff = idx[:, None] - idx[None, :]
    causal = (diff >= 0) & ((r[:, None] // seq_len) == (r[None, :] // seq_len))
    dm = jnp.where(causal[None], jnp.exp(jnp.where(causal, diff, 0.0)[None] * log_gamma[:, None, None]), 0.0)
    xi = jnp.exp((idx[:, None] + 1.0) * log_gamma[None, :])
    zeta = jnp.exp((seq_len - 1.0 - idx[:, None]) * log_gamma[None, :])
    g_l = jnp.exp(seq_len * log_gamma)
    xi_t = jnp.broadcast_to(xi.T[:, :, None], (RET_H, LANE, LANE))
    zeta_t = jnp.repeat(zeta.reshape(LANE, RET_H // 2, 2), dk, axis=2).transpose(1, 0, 2)
    gl_t = jnp.broadcast_to(jnp.repeat(g_l.reshape(RET_H // 2, 2), dk, axis=1)[:, :, None],
                            (RET_H // 2, LANE, LANE))
    return cos_t, sin_t, dm, xi_t, zeta_t, gl_t


def _ret_tile(q, k, v, cos, sin, dm0, dm1, scale):
    lane = lax.broadcasted_iota(jnp.int32, q.shape, 1)
    first = (lane & (LANE // 4)) == 0

    def rot(x):
        sw = jnp.where(first, pltpu.roll(x, LANE - LANE // 4, axis=1), pltpu.roll(x, LANE // 4, axis=1))
        return x * cos + sw * sin

    q = rot(q) * scale
    k = rot(k)
    lo = lane < LANE // 2
    q0 = jnp.where(lo, q, 0.0)
    q1 = jnp.where(lo, 0.0, q)
    a0 = _dot_nt(q0, k) * dm0
    a1 = _dot_nt(q1, k) * dm1
    o0 = _dot(a0, v[:, :LANE])
    o1 = _dot(a1, v[:, LANE:])
    return k, q0, q1, o0, o1


def _ret_finish(o, gr, w, b):
    mu = jnp.mean(o, axis=-1, keepdims=True)
    c = o - mu
    y = c * lax.rsqrt(jnp.mean(c * c, axis=-1, keepdims=True) + GN_EPS)
    return ((y * w + b) * _silu(gr)).astype(BF16)


def _ret_prompt_kernel(q_ref, k_ref, v_ref, gr_ref, cos_ref, sin_ref, dm_ref, xi_ref, zeta_ref, gl_ref,
                       w_ref, b_ref, o_ref, s_ref, p_scr, *, scale, n_chunks):
    tb = pl.program_id(1)
    pairs = RET_H // 2
    half = LANE // 2

    @pl.when(tb == 0)
    def _():
        p_scr[...] = jnp.zeros_like(p_scr)

    for c in range(n_chunks):
        rows = pl.ds(c * LANE, LANE)
        cos, sin = cos_ref[rows, :], sin_ref[rows, :]
        for j in range(pairs):
            kl = slice(j * LANE, (j + 1) * LANE)
            v0l = slice(2 * j * LANE, (2 * j + 1) * LANE)
            v1l = slice((2 * j + 1) * LANE, (2 * j + 2) * LANE)
            vl = slice(2 * j * LANE, (2 * j + 2) * LANE)
            v = v_ref[rows, vl]
            k, q0, q1, o0, o1 = _ret_tile(q_ref[rows, kl], k_ref[rows, kl], v, cos, sin,
                                          dm_ref[2 * j], dm_ref[2 * j + 1], scale)
            p = p_scr[j]
            o0 = o0 + _dot(q0, p[:, :LANE]) * xi_ref[2 * j]
            o1 = o1 + _dot(q1, p[:, LANE:]) * xi_ref[2 * j + 1]
            gl = gl_ref[j]
            p_scr[j] = jnp.concatenate([gl, gl], axis=1) * p + _dot((k * zeta_ref[j]).T, v)
            o_ref[rows, v0l] = _ret_finish(o0, gr_ref[rows, v0l], w_ref[:, v0l], b_ref[:, v0l])
            o_ref[rows, v1l] = _ret_finish(o1, gr_ref[rows, v1l], w_ref[:, v1l], b_ref[:, v1l])

    @pl.when(tb == pl.num_programs(1) - 1)
    def _():
        for j in range(pairs):
            s_ref[2 * j] = p_scr[j, :half, :LANE]
            s_ref[2 * j + 1] = p_scr[j, half:, LANE:]


def _ret_prompt(z, ret_w, ret_b, layer, n_seq, seq_len, col0, tb):
    m = z.shape[0]
    nt = seq_len // tb
    pairs = RET_H // 2
    hk, hv = RET_H * LANE // 2, RET_H * LANE
    _, _, dm, xi_t, zeta_t, gl_t = _ret_tables(LANE, 0, LANE)
    cos_t, sin_t = _ret_tables(seq_len, 0, seq_len)[:2]
    kern = functools.partial(_ret_prompt_kernel, scale=(LANE // 2) ** -0.5, n_chunks=tb // LANE)
    row = lambda b, t: b * nt + t
    full = lambda a: pl.BlockSpec(a.shape, lambda b, t: (0,) * a.ndim)
    return pl.pallas_call(
        kern,
        grid=(n_seq, nt),
        in_specs=[
            pl.BlockSpec((tb, hk), lambda b, t: (row(b, t), col0 // hk)),
            pl.BlockSpec((tb, hk), lambda b, t: (row(b, t), (col0 + hk) // hk)),
            pl.BlockSpec((tb, hv), lambda b, t: (row(b, t), (col0 + 2 * hk) // hv)),
            pl.BlockSpec((tb, hv), lambda b, t: (row(b, t), (col0 + 2 * hk + hv) // hv)),
            pl.BlockSpec((tb, LANE), lambda b, t: (t, 0)),
            pl.BlockSpec((tb, LANE), lambda b, t: (t, 0)),
            full(dm), full(xi_t), full(zeta_t), full(gl_t),
            pl.BlockSpec((None, 1, hv), lambda b, t: (layer, 0, 0)),
            pl.BlockSpec((None, 1, hv), lambda b, t: (layer, 0, 0)),
        ],
        out_specs=[
            pl.BlockSpec((tb, hv), lambda b, t: (row(b, t), 0)),
            pl.BlockSpec((None, RET_H, LANE // 2, LANE), lambda b, t: (b, 0, 0, 0)),
        ],
        out_shape=[jax.ShapeDtypeStruct((m, hv), BF16),
                   jax.ShapeDtypeStruct((n_seq, RET_H, LANE // 2, LANE), F32)],
        scratch_shapes=[pltpu.VMEM((pairs, LANE, 2 * LANE), F32)],
        compiler_params=_params(("parallel", "arbitrary")),
        name="ret_prompt",
    )(z, z, z, z, cos_t, sin_t, dm, xi_t, zeta_t, gl_t, ret_w, ret_b)


def _ret_sample_kernel(*refs, scale, seq_len):
    (q_ref, k_ref, v_ref, gr_ref, cos_ref, sin_ref, dm_ref, xi_ref, zeta_ref, gl_ref,
     w_ref, b_ref, s_in_ref) = refs[:13]
    o_ref, s_out_ref = refs[-2:]
    n = LANE // seq_len
    sh = int(math.log2(seq_len))
    half = LANE // 2
    v = v_ref[...]
    k, q0, q1, o0, o1 = _ret_tile(q_ref[...], k_ref[...], v, cos_ref[...], sin_ref[...],
                                  dm_ref[0], dm_ref[1], scale)

    p4 = s_in_ref[...]
    p_bf = p4.reshape(n * LANE, LANE).astype(BF16)
    row = lax.broadcasted_iota(jnp.int32, q0.shape, 0)

    def spread(q):
        q_b = q.astype(BF16)
        return jnp.concatenate([jnp.where((row >> sh) == b, q_b, jnp.zeros_like(q_b)) for b in range(n)], axis=1)

    o0 = o0 + jnp.dot(spread(q0), p_bf, preferred_element_type=F32) * xi_ref[0]
    o1 = o1 + jnp.dot(spread(q1), p_bf, preferred_element_type=F32) * xi_ref[1]
    gr = gr_ref[...]
    o_ref[:, :LANE] = _ret_finish(o0, gr[:, :LANE], w_ref[:, :LANE], b_ref[:, :LANE])
    o_ref[:, LANE:] = _ret_finish(o1, gr[:, LANE:], w_ref[:, LANE:], b_ref[:, LANE:])

    kz_t = (k * zeta_ref[...]).T.astype(BF16)
    seq = lax.broadcasted_iota(jnp.int32, (n, LANE, LANE), 0)
    col = lax.broadcasted_iota(jnp.int32, (n, LANE, LANE), 2)
    kz_blk = jnp.where((col >> sh) == seq, kz_t[None], jnp.zeros((), BF16))
    upd = jnp.dot(kz_blk.reshape(n * LANE, LANE), v.astype(BF16), preferred_element_type=F32)
    upd = upd.reshape(n, 2, half, 2 * LANE)
    upd = jnp.concatenate([upd[:, 0:1, :, :LANE], upd[:, 1:2, :, LANE:]], axis=1)
    s_out_ref[...] = gl_ref[...].reshape(1, 2, half, LANE) * p4 + upd


def _ret_sample(z, ret_w, ret_b, state, o_full, s_full, layer, row0, n_seq, seq_len, col0):
    depth = state.shape[0]
    per_tile = LANE // seq_len
    pairs = RET_H // 2
    dk2, dv2 = LANE, 2 * LANE
    r0 = row0 // LANE
    tables = _ret_tables(seq_len, PAST_LEN, LANE)
    kern = functools.partial(_ret_sample_kernel, scale=(LANE // 2) ** -0.5, seq_len=seq_len)
    q0 = col0 // dk2
    k0 = (col0 + RET_H * LANE // 2) // dk2
    v0 = (col0 + RET_H * LANE) // dv2
    g0 = (col0 + 2 * RET_H * LANE) // dv2
    carried = [o_full] if s_full is None else [o_full, s_full]
    return pl.pallas_call(
        kern,
        grid=(n_seq // per_tile, pairs),
        in_specs=[
            pl.BlockSpec((LANE, dk2), lambda i, j: (r0 + i, q0 + j)),
            pl.BlockSpec((LANE, dk2), lambda i, j: (r0 + i, k0 + j)),
            pl.BlockSpec((LANE, dv2), lambda i, j: (r0 + i, v0 + j)),
            pl.BlockSpec((LANE, dv2), lambda i, j: (r0 + i, g0 + j)),
            pl.BlockSpec((LANE, LANE), lambda i, j: (0, 0)),
            pl.BlockSpec((LANE, LANE), lambda i, j: (0, 0)),
            pl.BlockSpec((2, LANE, LANE), lambda i, j: (j, 0, 0)),
            pl.BlockSpec((2, LANE, LANE), lambda i, j: (j, 0, 0)),
            pl.BlockSpec((None, LANE, LANE), lambda i, j: (j, 0, 0)),
            pl.BlockSpec((None, LANE, LANE), lambda i, j: (j, 0, 0)),
            pl.BlockSpec((None, 1, dv2), lambda i, j: (layer, 0, j)),
            pl.BlockSpec((None, 1, dv2), lambda i, j: (layer, 0, j)),
            pl.BlockSpec((None, per_tile, 2, LANE // 2, LANE), lambda i, j: (layer, i, j, 0, 0)),
        ] + [pl.BlockSpec(memory_space=pl.ANY)] * len(carried),
        out_specs=[
            pl.BlockSpec((LANE, dv2), lambda i, j: (r0 + i, j)),
            pl.BlockSpec((None, per_tile, 2, LANE // 2, LANE), lambda i, j: (layer, i, j, 0, 0)),
        ],
        out_shape=[jax.ShapeDtypeStruct(o_full.shape, BF16),
                   jax.ShapeDtypeStruct((depth, n_seq, RET_H, LANE // 2, LANE), F32)],
        input_output_aliases={13 + n: n for n in range(len(carried))},
        compiler_params=_params(("parallel", "parallel")),
        name="ret_sample",
    )(z, z, z, z, *tables, ret_w, ret_b, state, *carried)


def _conv_taps(cb, cc, ch, w_ref, prev1, prev2, seq_len):
    u = cc * ch
    t = lax.broadcasted_iota(jnp.int32, u.shape, 0) & (seq_len - 1)
    um1 = jnp.where(t >= 1, pltpu.roll(u, 1, axis=0), prev1)
    um2 = jnp.where(t >= 2, pltpu.roll(u, 2, axis=0), prev2)
    conv = w_ref[0:1, :] * um2 + w_ref[1:2, :] * um1 + w_ref[2:3, :] * u
    return (cb * conv).astype(BF16), u


def _conv_prompt_kernel(cb_ref, cc_ref, ch_ref, w_ref, o_ref, s_ref, *, seq_len):
    o, u = _conv_taps(cb_ref[...], cc_ref[...], ch_ref[...], w_ref, 0.0, 0.0, seq_len)
    o_ref[...] = o
    s_ref[...] = u[seq_len - (CONV_W - 1):, :]


def _conv_sample_kernel(cb_ref, cc_ref, ch_ref, w_ref, p1_ref, p2_ref, o_any, o_ref, u_ref, *, seq_len):
    del o_any
    o, u = _conv_taps(cb_ref[...], cc_ref[...], ch_ref[...], w_ref, p1_ref[...], p2_ref[...], seq_len)
    o_ref[...] = o
    u_ref[...] = u


def _conv_specs(rows_blk, cols_blk, r0, c0, nb, layer):
    return [
        pl.BlockSpec((rows_blk, cols_blk), lambda r, c: (r0 + r, c0 + c)),
        pl.BlockSpec((rows_blk, cols_blk), lambda r, c: (r0 + r, c0 + nb + c)),
        pl.BlockSpec((rows_blk, cols_blk), lambda r, c: (r0 + r, c0 + 2 * nb + c)),
        pl.BlockSpec((None, CONV_W, cols_blk), lambda r, c: (layer, 0, c)),
    ]


def _conv_prompt(z, conv_w, layer, n_seq, seq_len, col0, cdim, cols_blk):
    m = z.shape[0]
    nb = cdim // cols_blk
    kern = functools.partial(_conv_prompt_kernel, seq_len=seq_len)
    return pl.pallas_call(
        kern,
        grid=(n_seq, nb),
        in_specs=_conv_specs(seq_len, cols_blk, 0, col0 // cols_blk, nb, layer),
        out_specs=[pl.BlockSpec((seq_len, cols_blk), lambda r, c: (r, c)),
                   pl.BlockSpec((None, CONV_W - 1, cols_blk), lambda r, c: (r, 0, c))],
        out_shape=[jax.ShapeDtypeStruct((m, cdim), BF16),
                   jax.ShapeDtypeStruct((n_seq, CONV_W - 1, cdim), F32)],
        compiler_params=_params(("parallel", "parallel")),
        name="conv_prompt",
    )(z, z, z, conv_w)


def _conv_sample(z, conv_w, prev1, prev2, o_full, layer, row0, n_rows, seq_len, col0, cdim, cols_blk):
    nb = cdim // cols_blk
    r0 = row0 // n_rows
    kern = functools.partial(_conv_sample_kernel, seq_len=seq_len)
    return pl.pallas_call(
        kern,
        grid=(1, nb),
        in_specs=_conv_specs(n_rows, cols_blk, r0, col0 // cols_blk, nb, layer) + [
            pl.BlockSpec((n_rows, cols_blk), lambda r, c: (0, c)),
            pl.BlockSpec((n_rows, cols_blk), lambda r, c: (0, c)),
            pl.BlockSpec(memory_space=pl.ANY),
        ],
        out_specs=[pl.BlockSpec((n_rows, cols_blk), lambda r, c: (r0, c)),
                   pl.BlockSpec((n_rows, cols_blk), lambda r, c: (0, c))],
        out_shape=[jax.ShapeDtypeStruct(o_full.shape, BF16),
                   jax.ShapeDtypeStruct((n_rows, cdim), F32)],
        input_output_aliases={6: 0},
        compiler_params=_params(("parallel", "parallel")),
        name="conv_sample",
    )(z, z, z, conv_w, prev1, prev2, o_full)


def _merge_kernel(oa_ref, ob_ref, oc_ref, ga_ref, gb_ref, gc_ref, wa_ref, wb_ref, wc_ref, m_ref):
    m = jax.nn.sigmoid(ga_ref[...]) * _dot(oa_ref[...], wa_ref[...])
    m = m + jax.nn.sigmoid(gb_ref[...]) * _dot(ob_ref[...], wb_ref[...])
    m = m + jax.nn.sigmoid(gc_ref[...]) * _dot(oc_ref[...], wc_ref[...])
    m_ref[...] = m.astype(BF16)


def _merge(oa, ob, oc, z, wa, wb, wc, layer, gate_col0, d):
    m, kdim = oa.shape
    tn = TN_MERGE
    g0 = gate_col0 // tn
    nd = d // tn
    o_spec = pl.BlockSpec((TM, kdim), lambda i, j: (i, 0), pipeline_mode=SINGLE)
    w_spec = pl.BlockSpec((None, kdim, tn), lambda i, j: (layer, 0, j))
    return pl.pallas_call(
        _merge_kernel,
        grid=(m // TM, nd),
        in_specs=[o_spec, o_spec, o_spec,
                  pl.BlockSpec((TM, tn), lambda i, j: (i, g0 + j)),
                  pl.BlockSpec((TM, tn), lambda i, j: (i, g0 + nd + j)),
                  pl.BlockSpec((TM, tn), lambda i, j: (i, g0 + 2 * nd + j)),
                  w_spec, w_spec, w_spec],
        out_specs=pl.BlockSpec((TM, tn), lambda i, j: (i, j)),
        out_shape=jax.ShapeDtypeStruct((m, d), BF16),
        compiler_params=_params(("parallel", "arbitrary")),
        name="merge",
    )(oa, ob, oc, z, z, z, wa, wb, wc)


def _outproj_kernel(m_ref, w_ref, x_ref, o_ref):
    o_ref[...] = x_ref[...] + _dot(m_ref[...], w_ref[...])


def _outproj2_kernel(m_ref, w_ref, xp_ref, xs_ref, o_ref, *, n_first):
    i = pl.program_id(0)
    upd = _dot(m_ref[...], w_ref[...])

    @pl.when(i < pl.num_programs(0) - 1)
    def _():
        o_ref[...] = xp_ref[...] + upd

    @pl.when(i == pl.num_programs(0) - 1)
    def _():
        o_ref[:n_first, :] = xp_ref[:n_first, :] + upd[:n_first]
        o_ref[n_first:, :] = xs_ref[...] + upd[n_first:]


def _outproj(mm, w_out, xs, layer):
    m, d = mm.shape
    if len(xs) == 1:
        kern = _outproj_kernel
        x_specs = [pl.BlockSpec((TM, TN), lambda i, j: (i, j))]
    else:
        ms = xs[1].shape[0]
        kern = functools.partial(_outproj2_kernel, n_first=TM - ms)
        x_specs = [pl.BlockSpec((TM, TN), lambda i, j: (i, j)),
                   pl.BlockSpec((ms, TN), lambda i, j: (0, j))]
    return pl.pallas_call(
        kern,
        grid=(m // TM, d // TN),
        in_specs=[pl.BlockSpec((TM, d), lambda i, j: (i, 0), pipeline_mode=SINGLE),
                  pl.BlockSpec((None, d, TN), lambda i, j: (layer, 0, j))] + x_specs,
        out_specs=pl.BlockSpec((TM, TN), lambda i, j: (i, j)),
        out_shape=jax.ShapeDtypeStruct((m, d), F32),
        compiler_params=_params(("parallel", "arbitrary")),
        name="outproj",
    )(mm, w_out, *xs)


def _mlp_step(x_ref, g_ref, wu_ref, wd_ref, acc_ref, h_scr):
    f = pl.program_id(1)

    @pl.when(f == 0)
    def _():
        x = x_ref[...]
        h_scr[...] = _rmsnorm(x, g_ref[...]).astype(BF16)
        acc_ref[...] = x

    wu = wu_ref[...].astype(BF16)
    wd = wd_ref[...].astype(BF16)
    slab = acc_ref.shape[0] // MLP_SLABS
    for r in range(MLP_SLABS):
        rows = pl.ds(r * slab, slab)
        a = jnp.maximum(jnp.dot(h_scr[rows, :], wu, preferred_element_type=F32), 0.0)
        acc_ref[rows, :] += jnp.dot((a * a).astype(BF16), wd, preferred_element_type=F32)


def _mlp_kernel(x_ref, g_ref, wu_ref, wd_ref, o_ref, h_scr):
    _mlp_step(x_ref, g_ref, wu_ref, wd_ref, o_ref, h_scr)


def _mlp_final_kernel(x_ref, g_ref, wu_ref, wd_ref, gf_ref, yp_ref, ys_ref, h_scr, *, n_first):
    _mlp_step(x_ref, g_ref, wu_ref, wd_ref, yp_ref, h_scr)

    @pl.when(pl.program_id(1) == pl.num_programs(1) - 1)
    def _():
        y = _rmsnorm(yp_ref[...], gf_ref[...])
        yp_ref[...] = y

        @pl.when(pl.program_id(0) == pl.num_programs(0) - 1)
        def _():
            ys_ref[...] = y[n_first:]


def _mlp(x, norm_g, w_up, w_down, layer, tf, final=None):
    m, d = x.shape
    dff = w_up.shape[2]
    tm = TM_MLP
    in_specs = [pl.BlockSpec((tm, d), lambda i, f: (i, 0), pipeline_mode=SINGLE),
                pl.BlockSpec((None, 1, d), lambda i, f: (layer, 0, 0)),
                pl.BlockSpec((None, d, tf), lambda i, f: (layer, 0, f)),
                pl.BlockSpec((None, tf, d), lambda i, f: (layer, f, 0))]
    common = dict(grid=(m // tm, dff // tf), scratch_shapes=[pltpu.VMEM((tm, d), BF16)],
                  compiler_params=_params(("parallel", "arbitrary")))
    if final is None:
        return pl.pallas_call(
            _mlp_kernel, in_specs=in_specs,
            out_specs=pl.BlockSpec((tm, d), lambda i, f: (i, 0)),
            out_shape=jax.ShapeDtypeStruct((m, d), F32), name="mlp", **common,
        )(x, norm_g, w_up, w_down)
    g_final, mp, ms = final
    kern = functools.partial(_mlp_final_kernel, n_first=tm - ms)
    return pl.pallas_call(
        kern, in_specs=in_specs + [pl.BlockSpec((1, d), lambda i, f: (0, 0))],
        out_specs=[pl.BlockSpec((tm, d), lambda i, f: (i, 0)),
                   pl.BlockSpec((ms, d), lambda i, f: (0, 0))],
        out_shape=[jax.ShapeDtypeStruct((mp, d), F32), jax.ShapeDtypeStruct((ms, d), F32)],
        name="mlp_final", **common,
    )(x, norm_g, w_up, w_down, g_final)


def kernel(x_prompt, x_sample, state_gla, state_ret, state_conv, norm_mix, w_in, w_gate2, b_gate2, gla_norm, ret_norm_w, ret_norm_b, conv_w, w_branch_gla, w_branch_ret, w_branch_conv, w_out, norm_mlp, w_up, w_down, norm_final):
    bp, tp, d = x_prompt.shape
    bs, ts, _ = x_sample.shape
    depth = w_in.shape[0]
    dk, dv = state_gla.shape[3], state_gla.shape[4]
    cdim = state_conv.shape[3]
    mp, ms = bp * tp, bs * ts
    rank_col = 2 * GLA_H * dk + GLA_H * dv
    n_cols = w_in.shape[2] - GLA_RANK
    ret_col0 = rank_col + GLA_H * dv
    conv_col0 = ret_col0 + 3 * RET_H * LANE
    gate_col0 = conv_col0 + 3 * cdim

    r3 = lambda a: a.reshape(depth, 1, a.shape[-1])
    norm_mix3, norm_mlp3, b_gate23 = r3(norm_mix), r3(norm_mlp), r3(b_gate2)
    gla_norm3, ret_w3, ret_b3 = r3(gla_norm), r3(ret_norm_w), r3(ret_norm_b)
    w_gate2_pad = jnp.pad(w_gate2, ((0, 0), (0, LANE - GLA_RANK), (0, 0)))

    xs = [x_prompt.reshape(mp, d), x_sample.reshape(ms, d)]
    new_gla_p, new_ret_p, new_conv_p, new_conv_s = [], [], [], []
    new_gla_s = new_ret_s = None
    for l in range(depth):
        h, loga = _prenorm(xs, norm_mix3, w_in, w_gate2_pad, b_gate23, l, rank_col)
        z = _inproj(h, w_in, l, n_cols, rank_col)

        oa, sg_p = _gla_prompt(z, loga, gla_norm3, l, bp, tp, dk, dv, tb=LANE)
        oa, new_gla_s = _gla_sample(z, loga, gla_norm3, state_gla, oa, new_gla_s, l, mp, bs, ts, dk, dv)
        ob, sr_p = _ret_prompt(z, ret_w3, ret_b3, l, bp, tp, ret_col0, tb=LANE)
        ob, new_ret_s = _ret_sample(z, ret_w3, ret_b3, state_ret, ob, new_ret_s, l, mp, bs, ts, ret_col0)

        oc, sc_p = _conv_prompt(z, conv_w, l, bp, tp, conv_col0, cdim, 2 * LANE)
        sc = state_conv[l]
        pad_rows = jnp.zeros((bs, ts - 1, cdim), F32)
        prev1 = jnp.concatenate([sc[:, 1:2], pad_rows], axis=1).reshape(ms, cdim)
        prev2 = jnp.concatenate([sc, pad_rows[:, 1:]], axis=1).reshape(ms, cdim)
        oc, u_s = _conv_sample(z, conv_w, prev1, prev2, oc, l, mp, ms, ts, conv_col0, cdim, 2 * LANE)

        mm = _merge(oa, ob, oc, z, w_branch_gla, w_branch_ret, w_branch_conv, l, gate_col0, d)
        x = _outproj(mm, w_out, xs, l)
        if l + 1 < depth:
            xs = [_mlp(x, norm_mlp3, w_up, w_down, l, tf=TN)]
        else:
            y_p, y_s = _mlp(x, norm_mlp3, w_up, w_down, l, tf=TN, final=(norm_final.reshape(1, d), mp, ms))

        new_gla_p.append(sg_p)
        new_ret_p.append(sr_p)
        new_conv_p.append(sc_p)
        new_conv_s.append(u_s.reshape(bs, ts, cdim)[:, ts - (CONV_W - 1):])

    return (y_p.reshape(bp, tp, d), y_s.reshape(bs, ts, d),
            jnp.stack(new_gla_p), jnp.stack(new_ret_p), jnp.stack(new_conv_p),
            new_gla_s, new_ret_s, jnp.stack(new_conv_s))
```

```python
import functools
import math

import numpy as np
import jax
import jax.numpy as jnp
from jax import lax
from jax.experimental import pallas as pl
from jax.experimental.pallas import tpu as pltpu

F32 = jnp.float32
BF16 = jnp.bfloat16

GLA_H = 4
GLA_RANK = 16
GATE_TAU = 16.0
RET_H = 8
ROPE_BASE = 10000.0
CONV_W = 3
EPS = 1e-6
GN_EPS = 1e-5
PAST_LEN = 16384

LANE = 128
VMEM_LIMIT = 60 * 1024 * 1024
TM = 2176
TM_MLP = 1088
TN = 512
TN_MERGE = 256
TR = 512
MLP_SLABS = 4
SINGLE = pl.Buffered(1)


def _dot(a, b):
    return jnp.dot(a.astype(BF16), b.astype(BF16), preferred_element_type=F32)


def _dot_nt(a, b):
    return lax.dot_general(a.astype(BF16), b.astype(BF16), (((1,), (1,)), ((), ())),
                           preferred_element_type=F32)


def _rmsnorm(x, g):
    y = x * lax.rsqrt(jnp.mean(x * x, axis=-1, keepdims=True) + EPS)
    return y * g


def _silu(x):
    return x * jax.nn.sigmoid(x)


def _params(sem):
    return pltpu.CompilerParams(dimension_semantics=sem, vmem_limit_bytes=VMEM_LIMIT)


def _prenorm_body(x, g_ref, wrank_ref, wg2_ref, bg2_ref, h_ref, loga_ref):
    h = _rmsnorm(x, g_ref[...]).astype(BF16)
    h_ref[...] = h
    alr = _dot_nt(h, wrank_ref[...])
    pre = _dot(alr, wg2_ref[...]) + bg2_ref[...]
    loga_ref[...] = (jnp.minimum(pre, 0.0) - jnp.log1p(jnp.exp(-jnp.abs(pre)))) * (1.0 / GATE_TAU)


def _prenorm_kernel(x_ref, g_ref, wrank_ref, wg2_ref, bg2_ref, h_ref, loga_ref):
    _prenorm_body(x_ref[...], g_ref, wrank_ref, wg2_ref, bg2_ref, h_ref, loga_ref)


def _prenorm2_kernel(xp_ref, xs_ref, g_ref, wrank_ref, wg2_ref, bg2_ref, h_ref, loga_ref, *, n_first):
    i = pl.program_id(0)

    @pl.when(i < n_first)
    def _():
        _prenorm_body(xp_ref[...], g_ref, wrank_ref, wg2_ref, bg2_ref, h_ref, loga_ref)

    @pl.when(i >= n_first)
    def _():
        _prenorm_body(xs_ref[...], g_ref, wrank_ref, wg2_ref, bg2_ref, h_ref, loga_ref)


def _prenorm(xs, norm_g, w_in, w_gate2_pad, b_gate2, layer, rank_col):
    d = xs[0].shape[1]
    m = sum(x.shape[0] for x in xs)
    gk = w_gate2_pad.shape[2]
    if len(xs) == 1:
        kern = _prenorm_kernel
        x_specs = [pl.BlockSpec((TR, d), lambda i: (i, 0))]
    else:
        n_first = xs[0].shape[0] // TR
        kern = functools.partial(_prenorm2_kernel, n_first=n_first)
        x_specs = [pl.BlockSpec((TR, d), lambda i: (jnp.minimum(i, n_first - 1), 0)),
                   pl.BlockSpec((TR, d), lambda i: (jnp.maximum(i - n_first, 0), 0))]
    return pl.pallas_call(
        kern,
        grid=(m // TR,),
        in_specs=x_specs + [
            pl.BlockSpec((None, 1, d), lambda i: (layer, 0, 0)),
            pl.BlockSpec((None, LANE, d), lambda i: (layer, rank_col // LANE, 0)),
            pl.BlockSpec((None, LANE, gk), lambda i: (layer, 0, 0)),
            pl.BlockSpec((None, 1, gk), lambda i: (layer, 0, 0)),
        ],
        out_specs=[pl.BlockSpec((TR, d), lambda i: (i, 0)),
                   pl.BlockSpec((TR, gk), lambda i: (i, 0))],
        out_shape=[jax.ShapeDtypeStruct((m, d), BF16),
                   jax.ShapeDtypeStruct((m, gk), F32)],
        compiler_params=_params(("parallel",)),
        name="prenorm",
    )(*xs, norm_g, w_in, w_gate2_pad, b_gate2)


def _inproj_kernel(h_ref, wmain_ref, wext_ref, z_ref, *, n_plain, shift):
    j = pl.program_id(1)

    @pl.when(j < n_plain)
    def _():
        z_ref[...] = _dot_nt(h_ref[...], wmain_ref[...])

    @pl.when(j >= n_plain)
    def _():
        tn = wmain_ref.shape[0]
        w = jnp.concatenate([wmain_ref[...].astype(BF16), wext_ref[...].astype(BF16)], axis=0)
        z_ref[...] = _dot_nt(h_ref[...], w[shift:shift + tn, :])


def _inproj(h, w_in, layer, n_cols, rank_col):
    m, d = h.shape
    ext_per_tile = TN // LANE
    kern = functools.partial(_inproj_kernel, n_plain=rank_col // TN, shift=GLA_RANK)
    return pl.pallas_call(
        kern,
        grid=(m // TM, n_cols // TN),
        in_specs=[
            pl.BlockSpec((TM, d), lambda i, j: (i, 0)),
            pl.BlockSpec((None, TN, d), lambda i, j: (layer, j, 0)),
            pl.BlockSpec((None, LANE, d), lambda i, j: (layer, ext_per_tile * (j + 1), 0)),
        ],
        out_specs=pl.BlockSpec((TM, TN), lambda i, j: (i, j)),
        out_shape=jax.ShapeDtypeStruct((m, n_cols), F32),
        compiler_params=_params(("parallel", "arbitrary")),
        name="inproj",
    )(h, w_in, w_in)


def _block_sums(g, top):
    row = lax.broadcasted_iota(jnp.int32, g.shape, 0)
    n = g.shape[0]
    c, d, tot = g, jnp.zeros_like(g), g
    out = {1: (c, d)}
    m = 1
    while m < top:
        later = (row & m) != 0
        tot_prev = pltpu.roll(tot, m, axis=0)
        tot_next = pltpu.roll(tot, n - m, axis=0)
        c = c + jnp.where(later, tot_prev, 0.0)
        d = d + jnp.where(later, 0.0, tot_next)
        tot = tot + jnp.where(later, tot_prev, tot_next)
        m *= 2
        out[m] = (c, d)
    return out, tot


def _gla_intra(qe, k, v, sums, seq_len):
    t = lax.broadcasted_iota(jnp.int32, (LANE, LANE), 0)
    s = lax.broadcasted_iota(jnp.int32, (LANE, LANE), 1)
    a = jnp.where(t == s, _dot_nt(qe, k), 0.0)
    m = 1
    while m < seq_len:
        c, d = sums[m]
        km = k if m == 1 else k * jnp.exp(d)
        sh = int(math.log2(2 * m))
        mask = ((t >> sh) == (s >> sh)) & ((t & m) != 0) & ((s & m) == 0)
        a = jnp.where(mask, _dot_nt(qe * jnp.exp(c), km), a)
        m *= 2
    return _dot(a, v)


def _gla_finish(o, ra, gn):
    y = o * lax.rsqrt(jnp.mean(o * o, axis=-1, keepdims=True) + EPS)
    return ((y * gn) * _silu(ra)).astype(BF16)


def _gla_prompt_kernel(q_ref, k_ref, v_ref, ra_ref, g_ref, gn_ref, o_ref, s_ref, st_scr,
                       *, scale, n_chunks, dk, dv):
    tb = pl.program_id(1)

    @pl.when(tb == 0)
    def _():
        st_scr[...] = jnp.zeros_like(st_scr)

    for c in range(n_chunks):
        rows = pl.ds(c * LANE, LANE)
        sums_all, tot_all = _block_sums(g_ref[rows, :], LANE)
        for h in range(GLA_H):
            kl = slice(h * dk, (h + 1) * dk)
            vl = slice(h * dv, (h + 1) * dv)
            sums = {m: (cd[0][:, kl], cd[1][:, kl]) for m, cd in sums_all.items()}
            qe = q_ref[rows, kl] * scale
            k = k_ref[rows, kl]
            v = v_ref[rows, vl]
            c_seq, d_seq = sums[LANE]
            st = st_scr[h]
            o = _gla_intra(qe, k, v, sums, LANE) + _dot_nt(qe * jnp.exp(c_seq), st)
            st_scr[h] = st * jnp.exp(tot_all[0:1, kl]) + _dot(v.T, k * jnp.exp(d_seq))
            o_ref[rows, vl] = _gla_finish(o, ra_ref[rows, vl], gn_ref[:, vl])

    @pl.when(tb == pl.num_programs(1) - 1)
    def _():
        for h in range(GLA_H):
            s_ref[h] = st_scr[h].T


def _gla_prompt(z, loga, gla_norm, layer, n_seq, seq_len, dk, dv, tb):
    m = z.shape[0]
    nt = seq_len // tb
    hk, hv = GLA_H * dk, GLA_H * dv
    kern = functools.partial(_gla_prompt_kernel, scale=dk ** -0.5, n_chunks=tb // LANE, dk=dk, dv=dv)
    row = lambda b, t: b * nt + t
    return pl.pallas_call(
        kern,
        grid=(n_seq, nt),
        in_specs=[
            pl.BlockSpec((tb, hk), lambda b, t: (row(b, t), 0)),
            pl.BlockSpec((tb, hk), lambda b, t: (row(b, t), 1)),
            pl.BlockSpec((tb, hv), lambda b, t: (row(b, t), (2 * hk) // hv)),
            pl.BlockSpec((tb, hv), lambda b, t: (row(b, t), (2 * hk + hv) // hv)),
            pl.BlockSpec((tb, hk), lambda b, t: (row(b, t), 0)),
            pl.BlockSpec((None, 1, hv), lambda b, t: (layer, 0, 0)),
        ],
        out_specs=[
            pl.BlockSpec((tb, hv), lambda b, t: (row(b, t), 0)),
            pl.BlockSpec((None, GLA_H, dk, dv), lambda b, t: (b, 0, 0, 0)),
        ],
        out_shape=[jax.ShapeDtypeStruct((m, hv), BF16),
                   jax.ShapeDtypeStruct((n_seq, GLA_H, dk, dv), F32)],
        scratch_shapes=[pltpu.VMEM((GLA_H, dv, dk), F32)],
        compiler_params=_params(("parallel", "arbitrary")),
        name="gla_prompt",
    )(z, z, z, z, loga, gla_norm)


def _gla_sample_kernel(*refs, scale, seq_len):
    q_ref, k_ref, v_ref, ra_ref, g_ref, gn_ref, s_in_ref = refs[:7]
    o_ref, s_out_ref = refs[-2:]
    n = LANE // seq_len
    sh = int(math.log2(seq_len))
    sums, tot = _block_sums(g_ref[...], seq_len)
    c_seq, d_seq = sums[seq_len]
    qe = q_ref[...] * scale
    k = k_ref[...]
    v = v_ref[...]
    dk, dv = k.shape[1], v.shape[1]
    o = _gla_intra(qe, k, v, sums, seq_len)

    q_b = (qe * jnp.exp(c_seq)).astype(BF16)
    row = lax.broadcasted_iota(jnp.int32, q_b.shape, 0)
    lhs = jnp.concatenate([jnp.where((row >> sh) == b, q_b, jnp.zeros_like(q_b)) for b in range(n)], axis=1)
    s_all = s_in_ref[...]
    o = o + jnp.dot(lhs, s_all.reshape(n * dk, dv).astype(BF16), preferred_element_type=F32)
    o_ref[...] = _gla_finish(o, ra_ref[...], gn_ref[...])

    kd_t = (k * jnp.exp(d_seq)).T.astype(BF16)
    dec_t = jnp.exp(tot).T
    seq = lax.broadcasted_iota(jnp.int32, (n, dk, LANE), 0)
    col = lax.broadcasted_iota(jnp.int32, (n, dk, LANE), 2)
    kd_blk = jnp.where((col >> sh) == seq, kd_t[None], jnp.zeros((), BF16))
    upd = jnp.dot(kd_blk.reshape(n * dk, LANE), v.astype(BF16), preferred_element_type=F32)
    dcol = jnp.sum(jnp.where(col == seq * seq_len, dec_t[None], 0.0), axis=2, keepdims=True)
    s_out_ref[...] = dcol * s_all + upd.reshape(n, dk, dv)


def _gla_sample(z, loga, gla_norm, state, o_full, s_full, layer, row0, n_seq, seq_len, dk, dv):
    depth = state.shape[0]
    per_tile = LANE // seq_len
    r0 = row0 // LANE
    kern = functools.partial(_gla_sample_kernel, scale=dk ** -0.5, seq_len=seq_len)
    v_blk0 = (2 * GLA_H * dk) // dv
    ra_blk0 = (2 * GLA_H * dk + GLA_H * dv) // dv
    carried = [o_full] if s_full is None else [o_full, s_full]
    return pl.pallas_call(
        kern,
        grid=(n_seq // per_tile, GLA_H),
        in_specs=[
            pl.BlockSpec((LANE, dk), lambda i, h: (r0 + i, h)),
            pl.BlockSpec((LANE, dk), lambda i, h: (r0 + i, GLA_H + h)),
            pl.BlockSpec((LANE, dv), lambda i, h: (r0 + i, v_blk0 + h)),
            pl.BlockSpec((LANE, dv), lambda i, h: (r0 + i, ra_blk0 + h)),
            pl.BlockSpec((LANE, dk), lambda i, h: (r0 + i, h)),
            pl.BlockSpec((None, 1, dv), lambda i, h: (layer, 0, h)),
            pl.BlockSpec((None, per_tile, None, dk, dv), lambda i, h: (layer, i, h, 0, 0)),
        ] + [pl.BlockSpec(memory_space=pl.ANY)] * len(carried),
        out_specs=[
            pl.BlockSpec((LANE, dv), lambda i, h: (r0 + i, h)),
            pl.BlockSpec((None, per_tile, None, dk, dv), lambda i, h: (layer, i, h, 0, 0)),
        ],
        out_shape=[jax.ShapeDtypeStruct(o_full.shape, BF16),
                   jax.ShapeDtypeStruct((depth, n_seq, GLA_H, dk, dv), F32)],
        input_output_aliases={7 + n: n for n in range(len(carried))},
        compiler_params=_params(("parallel", "parallel")),
        name="gla_sample",
    )(z, z, z, z, loga, gla_norm, state, *carried)


def _ret_tables(seq_len, pos0, n_rows):
    dk = LANE // 2
    half = dk // 2
    log_gamma = jnp.log1p(-jnp.exp2(-5.0 - jnp.arange(RET_H, dtype=F32)))
    pos = (pos0 + (jnp.arange(n_rows) % seq_len)).astype(F32)
    inv_freq = ROPE_BASE ** (-jnp.arange(half, dtype=F32) / half)
    ang = pos[:, None] * inv_freq[None, :]
    cos, sin = jnp.cos(ang), jnp.sin(ang)
    cos_t = jnp.tile(cos, (1, 4))
    sin_t = jnp.tile(jnp.concatenate([-sin, sin], axis=1), (1, 2))
    r = jnp.arange(LANE)
    idx = (r % seq_len).astype(F32)
    diff = idx[:, None] - idx[None, :]
    causal = (diff >= 0) & ((r[:, None] // seq_len) == (r[None, :] // seq_len))
    dm = jnp.where(causal[None], jnp.exp(jnp.where(causal, diff, 0.0)[None] * log_gamma[:, None, None]), 0.0)
    xi = jnp.exp((idx[:, None] + 1.0) * log_gamma[None, :])
    zeta = jnp.exp((seq_len - 1.0 - idx[:, None]) * log_gamma[None, :])
    g_l = jnp.exp(seq_len * log_gamma)
    xi_t = jnp.broadcast_to(xi.T[:, :, None], (RET_H, LANE, LANE))
    zeta_t = jnp.repeat(zeta.reshape(LANE, RET_H // 2, 2), dk, axis=2).transpose(1, 0, 2)
    gl_t = jnp.broadcast_to(jnp.repeat(g_l.reshape(RET_H // 2, 2), dk, axis=1)[:, :, None],
                            (RET_H // 2, LANE, LANE))
    return cos_t, sin_t, dm, xi_t, zeta_t, gl_t


def _ret_tile(q, k, v, cos, sin, dm0, dm1, scale):
    lane = lax.broadcasted_iota(jnp.int32, q.shape, 1)
    first = (lane & (LANE // 4)) == 0

    def rot(x):
        sw = jnp.where(first, pltpu.roll(x, LANE - LANE // 4, axis=1), pltpu.roll(x, LANE // 4, axis=1))
        return x * cos + sw * sin

    q = rot(q) * scale
    k = rot(k)
    lo = lane < LANE // 2
    q0 = jnp.where(lo, q, 0.0)
    q1 = jnp.where(lo, 0.0, q)
    a0 = _dot_nt(q0, k) * dm0
    a1 = _dot_nt(q1, k) * dm1
    o0 = _dot(a0, v[:, :LANE])
    o1 = _dot(a1, v[:, LANE:])
    return k, q0, q1, o0, o1


def _ret_finish(o, gr, w, b):
    mu = jnp.mean(o, axis=-1, keepdims=True)
    c = o - mu
    y = c * lax.rsqrt(jnp.mean(c * c, axis=-1, keepdims=True) + GN_EPS)
    return ((y * w + b) * _silu(gr)).astype(BF16)


def _ret_prompt_kernel(q_ref, k_ref, v_ref, gr_ref, cos_ref, sin_ref, dm_ref, xi_ref, zeta_ref, gl_ref,
                       w_ref, b_ref, o_ref, s_ref, p_scr, *, scale, n_chunks):
    tb = pl.program_id(1)
    pairs = RET_H // 2
    half = LANE // 2

    @pl.when(tb == 0)
    def _():
        p_scr[...] = jnp.zeros_like(p_scr)

    for c in range(n_chunks):
        rows = pl.ds(c * LANE, LANE)
        cos, sin = cos_ref[rows, :], sin_ref[rows, :]
        for j in range(pairs):
            kl = slice(j * LANE, (j + 1) * LANE)
            v0l = slice(2 * j * LANE, (2 * j + 1) * LANE)
            v1l = slice((2 * j + 1) * LANE, (2 * j + 2) * LANE)
            vl = slice(2 * j * LANE, (2 * j + 2) * LANE)
            v = v_ref[rows, vl]
            k, q0, q1, o0, o1 = _ret_tile(q_ref[rows, kl], k_ref[rows, kl], v, cos, sin,
                                          dm_ref[2 * j], dm_ref[2 * j + 1], scale)
            p = p_scr[j]
            o0 = o0 + _dot(q0, p[:, :LANE]) * xi_ref[2 * j]
            o1 = o1 + _dot(q1, p[:, LANE:]) * xi_ref[2 * j + 1]
            gl = gl_ref[j]
            p_scr[j] = jnp.concatenate([gl, gl], axis=1) * p + _dot((k * zeta_ref[j]).T, v)
            o_ref[rows, v0l] = _ret_finish(o0, gr_ref[rows, v0l], w_ref[:, v0l], b_ref[:, v0l])
            o_ref[rows, v1l] = _ret_finish(o1, gr_ref[rows, v1l], w_ref[:, v1l], b_ref[:, v1l])

    @pl.when(tb == pl.num_programs(1) - 1)
    def _():
        for j in range(pairs):
            s_ref[2 * j] = p_scr[j, :half, :LANE]
            s_ref[2 * j + 1] = p_scr[j, half:, LANE:]


def _ret_prompt(z, ret_w, ret_b, layer, n_seq, seq_len, col0, tb):
    m = z.shape[0]
    nt = seq_len // tb
    pairs = RET_H // 2
    hk, hv = RET_H * LANE // 2, RET_H * LANE
    _, _, dm, xi_t, zeta_t, gl_t = _ret_tables(LANE, 0, LANE)
    cos_t, sin_t = _ret_tables(seq_len, 0, seq_len)[:2]
    kern = functools.partial(_ret_prompt_kernel, scale=(LANE // 2) ** -0.5, n_chunks=tb // LANE)
    row = lambda b, t: b * nt + t
    full = lambda a: pl.BlockSpec(a.shape, lambda b, t: (0,) * a.ndim)
    return pl.pallas_call(
        kern,
        grid=(n_seq, nt),
        in_specs=[
            pl.BlockSpec((tb, hk), lambda b, t: (row(b, t), col0 // hk)),
            pl.BlockSpec((tb, hk), lambda b, t: (row(b, t), (col0 + hk) // hk)),
            pl.BlockSpec((tb, hv), lambda b, t: (row(b, t), (col0 + 2 * hk) // hv)),
            pl.BlockSpec((tb, hv), lambda b, t: (row(b, t), (col0 + 2 * hk + hv) // hv)),
            pl.BlockSpec((tb, LANE), lambda b, t: (t, 0)),
            pl.BlockSpec((tb, LANE), lambda b, t: (t, 0)),
            full(dm), full(xi_t), full(zeta_t), full(gl_t),
            pl.BlockSpec((None, 1, hv), lambda b, t: (layer, 0, 0)),
            pl.BlockSpec((None, 1, hv), lambda b, t: (layer, 0, 0)),
        ],
        out_specs=[
            pl.BlockSpec((tb, hv), lambda b, t: (row(b, t), 0)),
            pl.BlockSpec((None, RET_H, LANE // 2, LANE), lambda b, t: (b, 0, 0, 0)),
        ],
        out_shape=[jax.ShapeDtypeStruct((m, hv), BF16),
                   jax.ShapeDtypeStruct((n_seq, RET_H, LANE // 2, LANE), F32)],
        scratch_shapes=[pltpu.VMEM((pairs, LANE, 2 * LANE), F32)],
        compiler_params=_params(("parallel", "arbitrary")),
        name="ret_prompt",
    )(z, z, z, z, cos_t, sin_t, dm, xi_t, zeta_t, gl_t, ret_w, ret_b)


def _ret_sample_kernel(*refs, scale, seq_len):
    (q_ref, k_ref, v_ref, gr_ref, cos_ref, sin_ref, dm_ref, xi_ref, zeta_ref, gl_ref,
     w_ref, b_ref, s_in_ref) = refs[:13]
    o_ref, s_out_ref = refs[-2:]
    n = LANE // seq_len
    sh = int(math.log2(seq_len))
    half = LANE // 2
    v = v_ref[...]
    k, q0, q1, o0, o1 = _ret_tile(q_ref[...], k_ref[...], v, cos_ref[...], sin_ref[...],
                                  dm_ref[0], dm_ref[1], scale)

    p4 = s_in_ref[...]
    p_bf = p4.reshape(n * LANE, LANE).astype(BF16)
    row = lax.broadcasted_iota(jnp.int32, q0.shape, 0)

    def spread(q):
        q_b = q.astype(BF16)
        return jnp.concatenate([jnp.where((row >> sh) == b, q_b, jnp.zeros_like(q_b)) for b in range(n)], axis=1)

    o0 = o0 + jnp.dot(spread(q0), p_bf, preferred_element_type=F32) * xi_ref[0]
    o1 = o1 + jnp.dot(spread(q1), p_bf, preferred_element_type=F32) * xi_ref[1]
    gr = gr_ref[...]
    o_ref[:, :LANE] = _ret_finish(o0, gr[:, :LANE], w_ref[:, :LANE], b_ref[:, :LANE])
    o_ref[:, LANE:] = _ret_finish(o1, gr[:, LANE:], w_ref[:, LANE:], b_ref[:, LANE:])

    kz_t = (k * zeta_ref[...]).T.astype(BF16)
    seq = lax.broadcasted_iota(jnp.int32, (n, LANE, LANE), 0)
    col = lax.broadcasted_iota(jnp.int32, (n, LANE, LANE), 2)
    kz_blk = jnp.where((col >> sh) == seq, kz_t[None], jnp.zeros((), BF16))
    upd = jnp.dot(kz_blk.reshape(n * LANE, LANE), v.astype(BF16), preferred_element_type=F32)
    upd = upd.reshape(n, 2, half, 2 * LANE)
    upd = jnp.concatenate([upd[:, 0:1, :, :LANE], upd[:, 1:2, :, LANE:]], axis=1)
    s_out_ref[...] = gl_ref[...].reshape(1, 2, half, LANE) * p4 + upd


def _ret_sample(z, ret_w, ret_b, state, o_full, s_full, layer, row0, n_seq, seq_len, col0):
    depth = state.shape[0]
    per_tile = LANE // seq_len
    pairs = RET_H // 2
    dk2, dv2 = LANE, 2 * LANE
    r0 = row0 // LANE
    tables = _ret_tables(seq_len, PAST_LEN, LANE)
    kern = functools.partial(_ret_sample_kernel, scale=(LANE // 2) ** -0.5, seq_len=seq_len)
    q0 = col0 // dk2
    k0 = (col0 + RET_H * LANE // 2) // dk2
    v0 = (col0 + RET_H * LANE) // dv2
    g0 = (col0 + 2 * RET_H * LANE) // dv2
    carried = [o_full] if s_full is None else [o_full, s_full]
    return pl.pallas_call(
        kern,
        grid=(n_seq // per_tile, pairs),
        in_specs=[
            pl.BlockSpec((LANE, dk2), lambda i, j: (r0 + i, q0 + j)),
            pl.BlockSpec((LANE, dk2), lambda i, j: (r0 + i, k0 + j)),
            pl.BlockSpec((LANE, dv2), lambda i, j: (r0 + i, v0 + j)),
            pl.BlockSpec((LANE, dv2), lambda i, j: (r0 + i, g0 + j)),
            pl.BlockSpec((LANE, LANE), lambda i, j: (0, 0)),
            pl.BlockSpec((LANE, LANE), lambda i, j: (0, 0)),
            pl.BlockSpec((2, LANE, LANE), lambda i, j: (j, 0, 0)),
            pl.BlockSpec((2, LANE, LANE), lambda i, j: (j, 0, 0)),
            pl.BlockSpec((None, LANE, LANE), lambda i, j: (j, 0, 0)),
            pl.BlockSpec((None, LANE, LANE), lambda i, j: (j, 0, 0)),
            pl.BlockSpec((None, 1, dv2), lambda i, j: (layer, 0, j)),
            pl.BlockSpec((None, 1, dv2), lambda i, j: (layer, 0, j)),
            pl.BlockSpec((None, per_tile, 2, LANE // 2, LANE), lambda i, j: (layer, i, j, 0, 0)),
        ] + [pl.BlockSpec(memory_space=pl.ANY)] * len(carried),
        out_specs=[
            pl.BlockSpec((LANE, dv2), lambda i, j: (r0 + i, j)),
            pl.BlockSpec((None, per_tile, 2, LANE // 2, LANE), lambda i, j: (layer, i, j, 0, 0)),
        ],
        out_shape=[jax.ShapeDtypeStruct(o_full.shape, BF16),
                   jax.ShapeDtypeStruct((depth, n_seq, RET_H, LANE // 2, LANE), F32)],
        input_output_aliases={13 + n: n for n in range(len(carried))},
        compiler_params=_params(("parallel", "parallel")),
        name="ret_sample",
    )(z, z, z, z, *tables, ret_w, ret_b, state, *carried)


def _conv_taps(cb, cc, ch, w_ref, prev1, prev2, seq_len):
    u = cc * ch
    t = lax.broadcasted_iota(jnp.int32, u.shape, 0) & (seq_len - 1)
    um1 = jnp.where(t >= 1, pltpu.roll(u, 1, axis=0), prev1)
    um2 = jnp.where(t >= 2, pltpu.roll(u, 2, axis=0), prev2)
    conv = w_ref[0:1, :] * um2 + w_ref[1:2, :] * um1 + w_ref[2:3, :] * u
    return (cb * conv).astype(BF16), u


def _conv_prompt_kernel(cb_ref, cc_ref, ch_ref, w_ref, o_ref, s_ref, *, seq_len):
    o, u = _conv_taps(cb_ref[...], cc_ref[...], ch_ref[...], w_ref, 0.0, 0.0, seq_len)
    o_ref[...] = o
    s_ref[...] = u[seq_len - (CONV_W - 1):, :]


def _conv_sample_kernel(cb_ref, cc_ref, ch_ref, w_ref, p1_ref, p2_ref, o_any, o_ref, u_ref, *, seq_len):
    del o_any
    o, u = _conv_taps(cb_ref[...], cc_ref[...], ch_ref[...], w_ref, p1_ref[...], p2_ref[...], seq_len)
    o_ref[...] = o
    u_ref[...] = u


def _conv_specs(rows_blk, cols_blk, r0, c0, nb, layer):
    return [
        pl.BlockSpec((rows_blk, cols_blk), lambda r, c: (r0 + r, c0 + c)),
        pl.BlockSpec((rows_blk, cols_blk), lambda r, c: (r0 + r, c0 + nb + c)),
        pl.BlockSpec((rows_blk, cols_blk), lambda r, c: (r0 + r, c0 + 2 * nb + c)),
        pl.BlockSpec((None, CONV_W, cols_blk), lambda r, c: (layer, 0, c)),
    ]


def _conv_prompt(z, conv_w, layer, n_seq, seq_len, col0, cdim, cols_blk):
    m = z.shape[0]
    nb = cdim // cols_blk
    kern = functools.partial(_conv_prompt_kernel, seq_len=seq_len)
    return pl.pallas_call(
        kern,
        grid=(n_seq, nb),
        in_specs=_conv_specs(seq_len, cols_blk, 0, col0 // cols_blk, nb, layer),
        out_specs=[pl.BlockSpec((seq_len, cols_blk), lambda r, c: (r, c)),
                   pl.BlockSpec((None, CONV_W - 1, cols_blk), lambda r, c: (r, 0, c))],
        out_shape=[jax.ShapeDtypeStruct((m, cdim), BF16),
                   jax.ShapeDtypeStruct((n_seq, CONV_W - 1, cdim), F32)],
        compiler_params=_params(("parallel", "parallel")),
        name="conv_prompt",
    )(z, z, z, conv_w)


def _conv_sample(z, conv_w, prev1, prev2, o_full, layer, row0, n_rows, seq_len, col0, cdim, cols_blk):
    nb = cdim // cols_blk
    r0 = row0 // n_rows
    kern = functools.partial(_conv_sample_kernel, seq_len=seq_len)
    return pl.pallas_call(
        kern,
        grid=(1, nb),
        in_specs=_conv_specs(n_rows, cols_blk, r0, col0 // cols_blk, nb, layer) + [
            pl.BlockSpec((n_rows, cols_blk), lambda r, c: (0, c)),
            pl.BlockSpec((n_rows, cols_blk), lambda r, c: (0, c)),
            pl.BlockSpec(memory_space=pl.ANY),
        ],
        out_specs=[pl.BlockSpec((n_rows, cols_blk), lambda r, c: (r0, c)),
                   pl.BlockSpec((n_rows, cols_blk), lambda r, c: (0, c))],
        out_shape=[jax.ShapeDtypeStruct(o_full.shape, BF16),
                   jax.ShapeDtypeStruct((n_rows, cdim), F32)],
        input_output_aliases={6: 0},
        compiler_params=_params(("parallel", "parallel")),
        name="conv_sample",
    )(z, z, z, conv_w, prev1, prev2, o_full)


def _merge_kernel(oa_ref, ob_ref, oc_ref, ga_ref, gb_ref, gc_ref, wa_ref, wb_ref, wc_ref, m_ref):
    m = jax.nn.sigmoid(ga_ref[...]) * _dot(oa_ref[...], wa_ref[...])
    m = m + jax.nn.sigmoid(gb_ref[...]) * _dot(ob_ref[...], wb_ref[...])
    m = m + jax.nn.sigmoid(gc_ref[...]) * _dot(oc_ref[...], wc_ref[...])
    m_ref[...] = m.astype(BF16)


def _merge(oa, ob, oc, z, wa, wb, wc, layer, gate_col0, d):
    m, kdim = oa.shape
    tn = TN_MERGE
    g0 = gate_col0 // tn
    nd = d // tn
    o_spec = pl.BlockSpec((TM, kdim), lambda i, j: (i, 0), pipeline_mode=SINGLE)
    w_spec = pl.BlockSpec((None, kdim, tn), lambda i, j: (layer, 0, j))
    return pl.pallas_call(
        _merge_kernel,
        grid=(m // TM, nd),
        in_specs=[o_spec, o_spec, o_spec,
                  pl.BlockSpec((TM, tn), lambda i, j: (i, g0 + j)),
                  pl.BlockSpec((TM, tn), lambda i, j: (i, g0 + nd + j)),
                  pl.BlockSpec((TM, tn), lambda i, j: (i, g0 + 2 * nd + j)),
                  w_spec, w_spec, w_spec],
        out_specs=pl.BlockSpec((TM, tn), lambda i, j: (i, j)),
        out_shape=jax.ShapeDtypeStruct((m, d), BF16),
        compiler_params=_params(("parallel", "arbitrary")),
        name="merge",
    )(oa, ob, oc, z, z, z, wa, wb, wc)


def _outproj_kernel(m_ref, w_ref, x_ref, o_ref):
    o_ref[...] = x_ref[...] + _dot(m_ref[...], w_ref[...])


def _outproj2_kernel(m_ref, w_ref, xp_ref, xs_ref, o_ref, *, n_first):
    i = pl.program_id(0)
    upd = _dot(m_ref[...], w_ref[...])

    @pl.when(i < pl.num_programs(0) - 1)
    def _():
        o_ref[...] = xp_ref[...] + upd

    @pl.when(i == pl.num_programs(0) - 1)
    def _():
        o_ref[:n_first, :] = xp_ref[:n_first, :] + upd[:n_first]
        o_ref[n_first:, :] = xs_ref[...] + upd[n_first:]


def _outproj(mm, w_out, xs, layer):
    m, d = mm.shape
    if len(xs) == 1:
        kern = _outproj_kernel
        x_specs = [pl.BlockSpec((TM, TN), lambda i, j: (i, j))]
    else:
        ms = xs[1].shape[0]
        kern = functools.partial(_outproj2_kernel, n_first=TM - ms)
        x_specs = [pl.BlockSpec((TM, TN), lambda i, j: (i, j)),
                   pl.BlockSpec((ms, TN), lambda i, j: (0, j))]
    return pl.pallas_call(
        kern,
        grid=(m // TM, d // TN),
        in_specs=[pl.BlockSpec((TM, d), lambda i, j: (i, 0), pipeline_mode=SINGLE),
                  pl.BlockSpec((None, d, TN), lambda i, j: (layer, 0, j))] + x_specs,
        out_specs=pl.BlockSpec((TM, TN), lambda i, j: (i, j)),
        out_shape=jax.ShapeDtypeStruct((m, d), F32),
        compiler_params=_params(("parallel", "arbitrary")),
        name="outproj",
    )(mm, w_out, *xs)


def _mlp_step(x_ref, g_ref, wu_ref, wd_ref, acc_ref, h_scr):
    f = pl.program_id(1)

    @pl.when(f == 0)
    def _():
        x = x_ref[...]
        h_scr[...] = _rmsnorm(x, g_ref[...]).astype(BF16)
        acc_ref[...] = x

    wu = wu_ref[...].astype(BF16)
    wd = wd_ref[...].astype(BF16)
    slab = acc_ref.shape[0] // MLP_SLABS
    for r in range(MLP_SLABS):
        rows = pl.ds(r * slab, slab)
        a = jnp.maximum(jnp.dot(h_scr[rows, :], wu, preferred_element_type=F32), 0.0)
        acc_ref[rows, :] += jnp.dot((a * a).astype(BF16), wd, preferred_element_type=F32)


def _mlp_kernel(x_ref, g_ref, wu_ref, wd_ref, o_ref, h_scr):
    _mlp_step(x_ref, g_ref, wu_ref, wd_ref, o_ref, h_scr)


def _mlp_final_kernel(x_ref, g_ref, wu_ref, wd_ref, gf_ref, yp_ref, ys_ref, h_scr, *, n_first):
    _mlp_step(x_ref, g_ref, wu_ref, wd_ref, yp_ref, h_scr)

    @pl.when(pl.program_id(1) == pl.num_programs(1) - 1)
    def _():
        y = _rmsnorm(yp_ref[...], gf_ref[...])
        yp_ref[...] = y

        @pl.when(pl.program_id(0) == pl.num_programs(0) - 1)
        def _():
            ys_ref[...] = y[n_first:]


def _mlp(x, norm_g, w_up, w_down, layer, tf, final=None):
    m, d = x.shape
    dff = w_up.shape[2]
    tm = TM_MLP
    in_specs = [pl.BlockSpec((tm, d), lambda i, f: (i, 0), pipeline_mode=SINGLE),
                pl.BlockSpec((None, 1, d), lambda i, f: (layer, 0, 0)),
                pl.BlockSpec((None, d, tf), lambda i, f: (layer, 0, f)),
                pl.BlockSpec((None, tf, d), lambda i, f: (layer, f, 0))]
    common = dict(grid=(m // tm, dff // tf), scratch_shapes=[pltpu.VMEM((tm, d), BF16)],
                  compiler_params=_params(("parallel", "arbitrary")))
    if final is None:
        return pl.pallas_call(
            _mlp_kernel, in_specs=in_specs,
            out_specs=pl.BlockSpec((tm, d), lambda i, f: (i, 0)),
            out_shape=jax.ShapeDtypeStruct((m, d), F32), name="mlp", **common,
        )(x, norm_g, w_up, w_down)
    g_final, mp, ms = final
    kern = functools.partial(_mlp_final_kernel, n_first=tm - ms)
    return pl.pallas_call(
        kern, in_specs=in_specs + [pl.BlockSpec((1, d), lambda i, f: (0, 0))],
        out_specs=[pl.BlockSpec((tm, d), lambda i, f: (i, 0)),
                   pl.BlockSpec((ms, d), lambda i, f: (0, 0))],
        out_shape=[jax.ShapeDtypeStruct((mp, d), F32), jax.ShapeDtypeStruct((ms, d), F32)],
        name="mlp_final", **common,
    )(x, norm_g, w_up, w_down, g_final)


def kernel(x_prompt, x_sample, state_gla, state_ret, state_conv, norm_mix, w_in, w_gate2, b_gate2, gla_norm, ret_norm_w, ret_norm_b, conv_w, w_branch_gla, w_branch_ret, w_branch_conv, w_out, norm_mlp, w_up, w_down, norm_final):
    bp, tp, d = x_prompt.shape
    bs, ts, _ = x_sample.shape
    depth = w_in.shape[0]
    dk, dv = state_gla.shape[3], state_gla.shape[4]
    cdim = state_conv.shape[3]
    mp, ms = bp * tp, bs * ts
    rank_col = 2 * GLA_H * dk + GLA_H * dv
    n_cols = w_in.shape[2] - GLA_RANK
    ret_col0 = rank_col + GLA_H * dv
    conv_col0 = ret_col0 + 3 * RET_H * LANE
    gate_col0 = conv_col0 + 3 * cdim

    r3 = lambda a: a.reshape(depth, 1, a.shape[-1])
    norm_mix3, norm_mlp3, b_gate23 = r3(norm_mix), r3(norm_mlp), r3(b_gate2)
    gla_norm3, ret_w3, ret_b3 = r3(gla_norm), r3(ret_norm_w), r3(ret_norm_b)
    w_gate2_pad = jnp.pad(w_gate2, ((0, 0), (0, LANE - GLA_RANK), (0, 0)))
    w_in_t = jnp.transpose(w_in, (0, 2, 1))

    xs = [x_prompt.reshape(mp, d), x_sample.reshape(ms, d)]
    new_gla_p, new_ret_p, new_conv_p, new_conv_s = [], [], [], []
    new_gla_s = new_ret_s = None
    for l in range(depth):
        h, loga = _prenorm(xs, norm_mix3, w_in_t, w_gate2_pad, b_gate23, l, rank_col)
        z = _inproj(h, w_in_t, l, n_cols, rank_col)

        oa, sg_p = _gla_prompt(z, loga, gla_norm3, l, bp, tp, dk, dv, tb=LANE)
        oa, new_gla_s = _gla_sample(z, loga, gla_norm3, state_gla, oa, new_gla_s, l, mp, bs, ts, dk, dv)
        ob, sr_p = _ret_prompt(z, ret_w3, ret_b3, l, bp, tp, ret_col0, tb=LANE)
        ob, new_ret_s = _ret_sample(z, ret_w3, ret_b3, state_ret, ob, new_ret_s, l, mp, bs, ts, ret_col0)

        oc, sc_p = _conv_prompt(z, conv_w, l, bp, tp, conv_col0, cdim, 2 * LANE)
        sc = state_conv[l]
        pad_rows = jnp.zeros((bs, ts - 1, cdim), F32)
        prev1 = jnp.concatenate([sc[:, 1:2], pad_rows], axis=1).reshape(ms, cdim)
        prev2 = jnp.concatenate([sc, pad_rows[:, 1:]], axis=1).reshape(ms, cdim)
        oc, u_s = _conv_sample(z, conv_w, prev1, prev2, oc, l, mp, ms, ts, conv_col0, cdim, 2 * LANE)

        mm = _merge(oa, ob, oc, z, w_branch_gla, w_branch_ret, w_branch_conv, l, gate_col0, d)
        x = _outproj(mm, w_out, xs, l)
        if l + 1 < depth:
            xs = [_mlp(x, norm_mlp3, w_up, w_down, l, tf=TN)]
        else:
            y_p, y_s = _mlp(x, norm_mlp3, w_up, w_down, l, tf=TN, final=(norm_final.reshape(1, d), mp, ms))

        new_gla_p.append(sg_p)
        new_ret_p.append(sr_p)
        new_conv_p.append(sc_p)
        new_conv_s.append(u_s.reshape(bs, ts, cdim)[:, ts - (CONV_W - 1):])

    return (y_p.reshape(bp, tp, d), y_s.reshape(bs, ts, d),
            jnp.stack(new_gla_p), jnp.stack(new_ret_p), jnp.stack(new_conv_p),
            new_gla_s, new_ret_s, jnp.stack(new_conv_s))
```

```python
import functools
import math

import numpy as np
import jax
import jax.numpy as jnp
from jax import lax
from jax.experimental import pallas as pl
from jax.experimental.pallas import tpu as pltpu

F32 = jnp.float32
BF16 = jnp.bfloat16

GLA_H = 4
GLA_RANK = 16
GATE_TAU = 16.0
RET_H = 8
ROPE_BASE = 10000.0
CONV_W = 3
EPS = 1e-6
GN_EPS = 1e-5
PAST_LEN = 16384

LANE = 128
VMEM_LIMIT = 60 * 1024 * 1024
TM = 2176
TM_MLP = 1088
TN = 512
TN_MERGE = 256
TR = 512
MLP_SLABS = 4
SINGLE = pl.Buffered(1)


def _dot(a, b):
    return jnp.dot(a.astype(BF16), b.astype(BF16), preferred_element_type=F32)


def _dot_nt(a, b):
    return lax.dot_general(a.astype(BF16), b.astype(BF16), (((1,), (1,)), ((), ())),
                           preferred_element_type=F32)


def _rmsnorm(x, g):
    y = x * lax.rsqrt(jnp.mean(x * x, axis=-1, keepdims=True) + EPS)
    return y * g


def _silu(x):
    return x * jax.nn.sigmoid(x)


def _params(sem):
    return pltpu.CompilerParams(dimension_semantics=sem, vmem_limit_bytes=VMEM_LIMIT)


def _prenorm_body(x, g_ref, wrank_ref, wg2_ref, bg2_ref, h_ref, loga_ref):
    h = _rmsnorm(x, g_ref[...]).astype(BF16)
    h_ref[...] = h
    alr = _dot_nt(h, wrank_ref[...])
    pre = _dot(alr, wg2_ref[...]) + bg2_ref[...]
    loga_ref[...] = (jnp.minimum(pre, 0.0) - jnp.log1p(jnp.exp(-jnp.abs(pre)))) * (1.0 / GATE_TAU)


def _prenorm_kernel(x_ref, g_ref, wrank_ref, wg2_ref, bg2_ref, h_ref, loga_ref):
    _prenorm_body(x_ref[...], g_ref, wrank_ref, wg2_ref, bg2_ref, h_ref, loga_ref)


def _prenorm2_kernel(xp_ref, xs_ref, g_ref, wrank_ref, wg2_ref, bg2_ref, h_ref, loga_ref, *, n_first):
    i = pl.program_id(0)

    @pl.when(i < n_first)
    def _():
        _prenorm_body(xp_ref[...], g_ref, wrank_ref, wg2_ref, bg2_ref, h_ref, loga_ref)

    @pl.when(i >= n_first)
    def _():
        _prenorm_body(xs_ref[...], g_ref, wrank_ref, wg2_ref, bg2_ref, h_ref, loga_ref)


def _prenorm(xs, norm_g, w_in, w_gate2_pad, b_gate2, layer, rank_col):
    d = xs[0].shape[1]
    m = sum(x.shape[0] for x in xs)
    gk = w_gate2_pad.shape[2]
    if len(xs) == 1:
        kern = _prenorm_kernel
        x_specs = [pl.BlockSpec((TR, d), lambda i: (i, 0))]
    else:
        n_first = xs[0].shape[0] // TR
        kern = functools.partial(_prenorm2_kernel, n_first=n_first)
        x_specs = [pl.BlockSpec((TR, d), lambda i: (jnp.minimum(i, n_first - 1), 0)),
                   pl.BlockSpec((TR, d), lambda i: (jnp.maximum(i - n_first, 0), 0))]
    return pl.pallas_call(
        kern,
        grid=(m // TR,),
        in_specs=x_specs + [
            pl.BlockSpec((None, 1, d), lambda i: (layer, 0, 0)),
            pl.BlockSpec((None, LANE, d), lambda i: (layer, rank_col // LANE, 0)),
            pl.BlockSpec((None, LANE, gk), lambda i: (layer, 0, 0)),
            pl.BlockSpec((None, 1, gk), lambda i: (layer, 0, 0)),
        ],
        out_specs=[pl.BlockSpec((TR, d), lambda i: (i, 0)),
                   pl.BlockSpec((TR, gk), lambda i: (i, 0))],
        out_shape=[jax.ShapeDtypeStruct((m, d), BF16),
                   jax.ShapeDtypeStruct((m, gk), F32)],
        compiler_params=_params(("parallel",)),
        name="prenorm",
    )(*xs, norm_g, w_in, w_gate2_pad, b_gate2)


def _inproj_kernel(h_ref, wmain_ref, wext_ref, z_ref, *, n_plain, shift):
    j = pl.program_id(1)

    @pl.when(j < n_plain)
    def _():
        z_ref[...] = _dot_nt(h_ref[...], wmain_ref[...])

    @pl.when(j >= n_plain)
    def _():
        tn = wmain_ref.shape[0]
        w = jnp.concatenate([wmain_ref[...].astype(BF16), wext_ref[...].astype(BF16)], axis=0)
        z_ref[...] = _dot_nt(h_ref[...], w[shift:shift + tn, :])


def _inproj(h, w_in, layer, n_cols, rank_col):
    m, d = h.shape
    ext_per_tile = TN // LANE
    kern = functools.partial(_inproj_kernel, n_plain=rank_col // TN, shift=GLA_RANK)
    return pl.pallas_call(
        kern,
        grid=(m // TM, n_cols // TN),
        in_specs=[
            pl.BlockSpec((TM, d), lambda i, j: (i, 0)),
            pl.BlockSpec((None, TN, d), lambda i, j: (layer, j, 0)),
            pl.BlockSpec((None, LANE, d), lambda i, j: (layer, ext_per_tile * (j + 1), 0)),
        ],
        out_specs=pl.BlockSpec((TM, TN), lambda i, j: (i, j)),
        out_shape=jax.ShapeDtypeStruct((m, n_cols), F32),
        compiler_params=_params(("parallel", "arbitrary")),
        name="inproj",
    )(h, w_in, w_in)


def _block_sums(g, top):
    row = lax.broadcasted_iota(jnp.int32, g.shape, 0)
    n = g.shape[0]
    c, d, tot = g, jnp.zeros_like(g), g
    out = {1: (c, d)}
    m = 1
    while m < top:
        later = (row & m) != 0
        tot_prev = pltpu.roll(tot, m, axis=0)
        tot_next = pltpu.roll(tot, n - m, axis=0)
        c = c + jnp.where(later, tot_prev, 0.0)
        d = d + jnp.where(later, 0.0, tot_next)
        tot = tot + jnp.where(later, tot_prev, tot_next)
        m *= 2
        out[m] = (c, d)
    return out, tot


def _gla_intra(qe, k, v, sums, seq_len):
    t = lax.broadcasted_iota(jnp.int32, (LANE, LANE), 0)
    s = lax.broadcasted_iota(jnp.int32, (LANE, LANE), 1)
    a = jnp.where(t == s, _dot_nt(qe, k), 0.0)
    m = 1
    while m < seq_len:
        c, d = sums[m]
        km = k if m == 1 else k * jnp.exp(d)
        sh = int(math.log2(2 * m))
        mask = ((t >> sh) == (s >> sh)) & ((t & m) != 0) & ((s & m) == 0)
        a = jnp.where(mask, _dot_nt(qe * jnp.exp(c), km), a)
        m *= 2
    return _dot(a, v)


def _gla_finish(o, ra, gn):
    y = o * lax.rsqrt(jnp.mean(o * o, axis=-1, keepdims=True) + EPS)
    return ((y * gn) * _silu(ra)).astype(BF16)


def _gla_prompt_kernel(q_ref, k_ref, v_ref, ra_ref, g_ref, gn_ref, o_ref, s_ref, st_scr,
                       *, scale, n_chunks, dk, dv):
    tb = pl.program_id(1)

    @pl.when(tb == 0)
    def _():
        st_scr[...] = jnp.zeros_like(st_scr)

    for c in range(n_chunks):
        rows = pl.ds(c * LANE, LANE)
        sums_all, tot_all = _block_sums(g_ref[rows, :], LANE)
        for h in range(GLA_H):
            kl = slice(h * dk, (h + 1) * dk)
            vl = slice(h * dv, (h + 1) * dv)
            sums = {m: (cd[0][:, kl], cd[1][:, kl]) for m, cd in sums_all.items()}
            qe = q_ref[rows, kl] * scale
            k = k_ref[rows, kl]
            v = v_ref[rows, vl]
            c_seq, d_seq = sums[LANE]
            st = st_scr[h]
            o = _gla_intra(qe, k, v, sums, LANE) + _dot_nt(qe * jnp.exp(c_seq), st)
            st_scr[h] = st * jnp.exp(tot_all[0:1, kl]) + _dot(v.T, k * jnp.exp(d_seq))
            o_ref[rows, vl] = _gla_finish(o, ra_ref[rows, vl], gn_ref[:, vl])

    @pl.when(tb == pl.num_programs(1) - 1)
    def _():
        for h in range(GLA_H):
            s_ref[h] = st_scr[h].T


def _gla_prompt(z, loga, gla_norm, layer, n_seq, seq_len, dk, dv, tb):
    m = z.shape[0]
    nt = seq_len // tb
    hk, hv = GLA_H * dk, GLA_H * dv
    kern = functools.partial(_gla_prompt_kernel, scale=dk ** -0.5, n_chunks=tb // LANE, dk=dk, dv=dv)
    row = lambda b, t: b * nt + t
    return pl.pallas_call(
        kern,
        grid=(n_seq, nt),
        in_specs=[
            pl.BlockSpec((tb, hk), lambda b, t: (row(b, t), 0)),
            pl.BlockSpec((tb, hk), lambda b, t: (row(b, t), 1)),
            pl.BlockSpec((tb, hv), lambda b, t: (row(b, t), (2 * hk) // hv)),
            pl.BlockSpec((tb, hv), lambda b, t: (row(b, t), (2 * hk + hv) // hv)),
            pl.BlockSpec((tb, hk), lambda b, t: (row(b, t), 0)),
            pl.BlockSpec((None, 1, hv), lambda b, t: (layer, 0, 0)),
        ],
        out_specs=[
            pl.BlockSpec((tb, hv), lambda b, t: (row(b, t), 0)),
            pl.BlockSpec((None, GLA_H, dk, dv), lambda b, t: (b, 0, 0, 0)),
        ],
        out_shape=[jax.ShapeDtypeStruct((m, hv), BF16),
                   jax.ShapeDtypeStruct((n_seq, GLA_H, dk, dv), F32)],
        scratch_shapes=[pltpu.VMEM((GLA_H, dv, dk), F32)],
        compiler_params=_params(("parallel", "arbitrary")),
        name="gla_prompt",
    )(z, z, z, z, loga, gla_norm)


def _gla_sample_kernel(*refs, scale, seq_len):
    q_ref, k_ref, v_ref, ra_ref, g_ref, gn_ref, s_in_ref = refs[:7]
    o_ref, s_out_ref = refs[-2:]
    n = LANE // seq_len
    sh = int(math.log2(seq_len))
    sums, tot = _block_sums(g_ref[...], seq_len)
    c_seq, d_seq = sums[seq_len]
    qe = q_ref[...] * scale
    k = k_ref[...]
    v = v_ref[...]
    dk, dv = k.shape[1], v.shape[1]
    o = _gla_intra(qe, k, v, sums, seq_len)

    q_b = (qe * jnp.exp(c_seq)).astype(BF16)
    row = lax.broadcasted_iota(jnp.int32, q_b.shape, 0)
    lhs = jnp.concatenate([jnp.where((row >> sh) == b, q_b, jnp.zeros_like(q_b)) for b in range(n)], axis=1)
    s_all = s_in_ref[...]
    o = o + jnp.dot(lhs, s_all.reshape(n * dk, dv).astype(BF16), preferred_element_type=F32)
    o_ref[...] = _gla_finish(o, ra_ref[...], gn_ref[...])

    kd_t = (k * jnp.exp(d_seq)).T.astype(BF16)
    dec_t = jnp.exp(tot).T
    seq = lax.broadcasted_iota(jnp.int32, (n, dk, LANE), 0)
    col = lax.broadcasted_iota(jnp.int32, (n, dk, LANE), 2)
    kd_blk = jnp.where((col >> sh) == seq, kd_t[None], jnp.zeros((), BF16))
    upd = jnp.dot(kd_blk.reshape(n * dk, LANE), v.astype(BF16), preferred_element_type=F32)
    dcol = jnp.sum(jnp.where(col == seq * seq_len, dec_t[None], 0.0), axis=2, keepdims=True)
    s_out_ref[...] = dcol * s_all + upd.reshape(n, dk, dv)


def _gla_sample(z, loga, gla_norm, state, o_full, s_full, layer, row0, n_seq, seq_len, dk, dv):
    depth = state.shape[0]
    per_tile = LANE // seq_len
    r0 = row0 // LANE
    kern = functools.partial(_gla_sample_kernel, scale=dk ** -0.5, seq_len=seq_len)
    v_blk0 = (2 * GLA_H * dk) // dv
    ra_blk0 = (2 * GLA_H * dk + GLA_H * dv) // dv
    carried = [o_full] if s_full is None else [o_full, s_full]
    return pl.pallas_call(
        kern,
        grid=(n_seq // per_tile, GLA_H),
        in_specs=[
            pl.BlockSpec((LANE, dk), lambda i, h: (r0 + i, h)),
            pl.BlockSpec((LANE, dk), lambda i, h: (r0 + i, GLA_H + h)),
            pl.BlockSpec((LANE, dv), lambda i, h: (r0 + i, v_blk0 + h)),
            pl.BlockSpec((LANE, dv), lambda i, h: (r0 + i, ra_blk0 + h)),
            pl.BlockSpec((LANE, dk), lambda i, h: (r0 + i, h)),
            pl.BlockSpec((None, 1, dv), lambda i, h: (layer, 0, h)),
            pl.BlockSpec((None, per_tile, None, dk, dv), lambda i, h: (layer, i, h, 0, 0)),
        ] + [pl.BlockSpec(memory_space=pl.ANY)] * len(carried),
        out_specs=[
            pl.BlockSpec((LANE, dv), lambda i, h: (r0 + i, h)),
            pl.BlockSpec((None, per_tile, None, dk, dv), lambda i, h: (layer, i, h, 0, 0)),
        ],
        out_shape=[jax.ShapeDtypeStruct(o_full.shape, BF16),
                   jax.ShapeDtypeStruct((depth, n_seq, GLA_H, dk, dv), F32)],
        input_output_aliases={7 + n: n for n in range(len(carried))},
        compiler_params=_params(("parallel", "parallel")),
        name="gla_sample",
    )(z, z, z, z, loga, gla_norm, state, *carried)


def _ret_tables(seq_len, pos0, n_rows):
    dk = LANE // 2
    half = dk // 2
    log_gamma = jnp.log1p(-jnp.exp2(-5.0 - jnp.arange(RET_H, dtype=F32)))
    pos = (pos0 + (jnp.arange(n_rows) % seq_len)).astype(F32)
    inv_freq = ROPE_BASE ** (-jnp.arange(half, dtype=F32) / half)
    ang = pos[:, None] * inv_freq[None, :]
    cos, sin = jnp.cos(ang), jnp.sin(ang)
    cos_t = jnp.tile(cos, (1, 4))
    sin_t = jnp.tile(jnp.concatenate([-sin, sin], axis=1), (1, 2))
    r = jnp.arange(LANE)
    idx = (r % seq_len).astype(F32)
    diff = idx[:, None] - idx[None, :]
    causal = (diff >= 0) & ((r[:, None] // seq_len) == (r[None, :] // seq_len))
    dm = jnp.where(causal[None], jnp.exp(jnp.where(causal, diff, 0.0)[None] * log_gamma[:, None, None]), 0.0)
    xi = jnp.exp((idx[:, None] + 1.0) * log_gamma[None, :])
    zeta = jnp.exp((seq_len - 1.0 - idx[:, None]) * log_gamma[None, :])
    g_l = jnp.exp(seq_len * log_gamma)
    xi_t = jnp.broadcast_to(xi.T[:, :, None], (RET_H, LANE, LANE))
    zeta_t = jnp.repeat(zeta.reshape(LANE, RET_H // 2, 2), dk, axis=2).transpose(1, 0, 2)
    gl_t = jnp.broadcast_to(jnp.repeat(g_l.reshape(RET_H // 2, 2), dk, axis=1)[:, :, None],
                            (RET_H // 2, LANE, LANE))
    return cos_t, sin_t, dm, xi_t, zeta_t, gl_t


def _ret_tile(q, k, v, cos, sin, dm0, dm1, scale):
    lane = lax.broadcasted_iota(jnp.int32, q.shape, 1)
    first = (lane & (LANE // 4)) == 0

    def rot(x):
        sw = jnp.where(first, pltpu.roll(x, LANE - LANE // 4, axis=1), pltpu.roll(x, LANE // 4, axis=1))
        return x * cos + sw * sin

    q = rot(q) * scale
    k = rot(k)
    lo = lane < LANE // 2
    q0 = jnp.where(lo, q, 0.0)
    q1 = jnp.where(lo, 0.0, q)
    a0 = _dot_nt(q0, k) * dm0
    a1 = _dot_nt(q1, k) * dm1
    o0 = _dot(a0, v[:, :LANE])
    o1 = _dot(a1, v[:, LANE:])
    return k, q0, q1, o0, o1


def _ret_finish(o, gr, w, b):
    mu = jnp.mean(o, axis=-1, keepdims=True)
    c = o - mu
    y = c * lax.rsqrt(jnp.mean(c * c, axis=-1, keepdims=True) + GN_EPS)
    return ((y * w + b) * _silu(gr)).astype(BF16)


def _ret_prompt_kernel(q_ref, k_ref, v_ref, gr_ref, cos_ref, sin_ref, dm_ref, xi_ref, zeta_ref, gl_ref,
                       w_ref, b_ref, o_ref, s_ref, p_scr, *, scale, n_chunks):
    tb = pl.program_id(1)
    pairs = RET_H // 2
    half = LANE // 2

    @pl.when(tb == 0)
    def _():
        p_scr[...] = jnp.zeros_like(p_scr)

    for c in range(n_chunks):
        rows = pl.ds(c * LANE, LANE)
        cos, sin = cos_ref[rows, :], sin_ref[rows, :]
        for j in range(pairs):
            kl = slice(j * LANE, (j + 1) * LANE)
            v0l = slice(2 * j * LANE, (2 * j + 1) * LANE)
            v1l = slice((2 * j + 1) * LANE, (2 * j + 2) * LANE)
            vl = slice(2 * j * LANE, (2 * j + 2) * LANE)
            v = v_ref[rows, vl]
            k, q0, q1, o0, o1 = _ret_tile(q_ref[rows, kl], k_ref[rows, kl], v, cos, sin,
                                          dm_ref[2 * j], dm_ref[2 * j + 1], scale)
            p = p_scr[j]
            o0 = o0 + _dot(q0, p[:, :LANE]) * xi_ref[2 * j]
            o1 = o1 + _dot(q1, p[:, LANE:]) * xi_ref[2 * j + 1]
            gl = gl_ref[j]
            p_scr[j] = jnp.concatenate([gl, gl], axis=1) * p + _dot((k * zeta_ref[j]).T, v)
            o_ref[rows, v0l] = _ret_finish(o0, gr_ref[rows, v0l], w_ref[:, v0l], b_ref[:, v0l])
            o_ref[rows, v1l] = _ret_finish(o1, gr_ref[rows, v1l], w_ref[:, v1l], b_ref[:, v1l])

    @pl.when(tb == pl.num_programs(1) - 1)
    def _():
        for j in range(pairs):
            s_ref[2 * j] = p_scr[j, :half, :LANE]
            s_ref[2 * j + 1] = p_scr[j, half:, LANE:]


def _ret_prompt(z, ret_w, ret_b, layer, n_seq, seq_len, col0, tb):
    m = z.shape[0]
    nt = seq_len // tb
    pairs = RET_H // 2
    hk, hv = RET_H * LANE // 2, RET_H * LANE
    _, _, dm, xi_t, zeta_t, gl_t = _ret_tables(LANE, 0, LANE)
    cos_t, sin_t = _ret_tables(seq_len, 0, seq_len)[:2]
    kern = functools.partial(_ret_prompt_kernel, scale=(LANE // 2) ** -0.5, n_chunks=tb // LANE)
    row = lambda b, t: b * nt + t
    full = lambda a: pl.BlockSpec(a.shape, lambda b, t: (0,) * a.ndim)
    return pl.pallas_call(
        kern,
        grid=(n_seq, nt),
        in_specs=[
            pl.BlockSpec((tb, hk), lambda b, t: (row(b, t), col0 // hk)),
            pl.BlockSpec((tb, hk), lambda b, t: (row(b, t), (col0 + hk) // hk)),
            pl.BlockSpec((tb, hv), lambda b, t: (row(b, t), (col0 + 2 * hk) // hv)),
            pl.BlockSpec((tb, hv), lambda b, t: (row(b, t), (col0 + 2 * hk + hv) // hv)),
            pl.BlockSpec((tb, LANE), lambda b, t: (t, 0)),
            pl.BlockSpec((tb, LANE), lambda b, t: (t, 0)),
            full(dm), full(xi_t), full(zeta_t), full(gl_t),
            pl.BlockSpec((None, 1, hv), lambda b, t: (layer, 0, 0)),
            pl.BlockSpec((None, 1, hv), lambda b, t: (layer, 0, 0)),
        ],
        out_specs=[
            pl.BlockSpec((tb, hv), lambda b, t: (row(b, t), 0)),
            pl.BlockSpec((None, RET_H, LANE // 2, LANE), lambda b, t: (b, 0, 0, 0)),
        ],
        out_shape=[jax.ShapeDtypeStruct((m, hv), BF16),
                   jax.ShapeDtypeStruct((n_seq, RET_H, LANE // 2, LANE), F32)],
        scratch_shapes=[pltpu.VMEM((pairs, LANE, 2 * LANE), F32)],
        compiler_params=_params(("parallel", "arbitrary")),
        name="ret_prompt",
    )(z, z, z, z, cos_t, sin_t, dm, xi_t, zeta_t, gl_t, ret_w, ret_b)


def _ret_sample_kernel(*refs, scale, seq_len):
    (q_ref, k_ref, v_ref, gr_ref, cos_ref, sin_ref, dm_ref, xi_ref, zeta_ref, gl_ref,
     w_ref, b_ref, s_in_ref) = refs[:13]
    o_ref, s_out_ref = refs[-2:]
    n = LANE // seq_len
    sh = int(math.log2(seq_len))
    half = LANE // 2
    v = v_ref[...]
    k, q0, q1, o0, o1 = _ret_tile(q_ref[...], k_ref[...], v, cos_ref[...], sin_ref[...],
                                  dm_ref[0], dm_ref[1], scale)

    p4 = s_in_ref[...]
    p_bf = p4.reshape(n * LANE, LANE).astype(BF16)
    row = lax.broadcasted_iota(jnp.int32, q0.shape, 0)

    def spread(q):
        q_b = q.astype(BF16)
        return jnp.concatenate([jnp.where((row >> sh) == b, q_b, jnp.zeros_like(q_b)) for b in range(n)], axis=1)

    o0 = o0 + jnp.dot(spread(q0), p_bf, preferred_element_type=F32) * xi_ref[0]
    o1 = o1 + jnp.dot(spread(q1), p_bf, preferred_element_type=F32) * xi_ref[1]
    gr = gr_ref[...]
    o_ref[:, :LANE] = _ret_finish(o0, gr[:, :LANE], w_ref[:, :LANE], b_ref[:, :LANE])
    o_ref[:, LANE:] = _ret_finish(o1, gr[:, LANE:], w_ref[:, LANE:], b_ref[:, LANE:])

    kz_t = (k * zeta_ref[...]).T.astype(BF16)
    seq = lax.broadcasted_iota(jnp.int32, (n, LANE, LANE), 0)
    col = lax.broadcasted_iota(jnp.int32, (n, LANE, LANE), 2)
    kz_blk = jnp.where((col >> sh) == seq, kz_t[None], jnp.zeros((), BF16))
    upd = jnp.dot(kz_blk.reshape(n * LANE, LANE), v.astype(BF16), preferred_element_type=F32)
    upd = upd.reshape(n, 2, half, 2 * LANE)
    upd = jnp.concatenate([upd[:, 0:1, :, :LANE], upd[:, 1:2, :, LANE:]], axis=1)
    s_out_ref[...] = gl_ref[...].reshape(1, 2, half, LANE) * p4 + upd


def _ret_sample(z, ret_w, ret_b, state, o_full, s_full, layer, row0, n_seq, seq_len, col0):
    depth = state.shape[0]
    per_tile = LANE // seq_len
    pairs = RET_H // 2
    dk2, dv2 = LANE, 2 * LANE
    r0 = row0 // LANE
    tables = _ret_tables(seq_len, PAST_LEN, LANE)
    kern = functools.partial(_ret_sample_kernel, scale=(LANE // 2) ** -0.5, seq_len=seq_len)
    q0 = col0 // dk2
    k0 = (col0 + RET_H * LANE // 2) // dk2
    v0 = (col0 + RET_H * LANE) // dv2
    g0 = (col0 + 2 * RET_H * LANE) // dv2
    carried = [o_full] if s_full is None else [o_full, s_full]
    return pl.pallas_call(
        kern,
        grid=(n_seq // per_tile, pairs),
        in_specs=[
            pl.BlockSpec((LANE, dk2), lambda i, j: (r0 + i, q0 + j)),
            pl.BlockSpec((LANE, dk2), lambda i, j: (r0 + i, k0 + j)),
            pl.BlockSpec((LANE, dv2), lambda i, j: (r0 + i, v0 + j)),
            pl.BlockSpec((LANE, dv2), lambda i, j: (r0 + i, g0 + j)),
            pl.BlockSpec((LANE, LANE), lambda i, j: (0, 0)),
            pl.BlockSpec((LANE, LANE), lambda i, j: (0, 0)),
            pl.BlockSpec((2, LANE, LANE), lambda i, j: (j, 0, 0)),
            pl.BlockSpec((2, LANE, LANE), lambda i, j: (j, 0, 0)),
            pl.BlockSpec((None, LANE, LANE), lambda i, j: (j, 0, 0)),
            pl.BlockSpec((None, LANE, LANE), lambda i, j: (j, 0, 0)),
            pl.BlockSpec((None, 1, dv2), lambda i, j: (layer, 0, j)),
            pl.BlockSpec((None, 1, dv2), lambda i, j: (layer, 0, j)),
            pl.BlockSpec((None, per_tile, 2, LANE // 2, LANE), lambda i, j: (layer, i, j, 0, 0)),
        ] + [pl.BlockSpec(memory_space=pl.ANY)] * len(carried),
        out_specs=[
            pl.BlockSpec((LANE, dv2), lambda i, j: (r0 + i, j)),
            pl.BlockSpec((None, per_tile, 2, LANE // 2, LANE), lambda i, j: (layer, i, j, 0, 0)),
        ],
        out_shape=[jax.ShapeDtypeStruct(o_full.shape, BF16),
                   jax.ShapeDtypeStruct((depth, n_seq, RET_H, LANE // 2, LANE), F32)],
        input_output_aliases={13 + n: n for n in range(len(carried))},
        compiler_params=_params(("parallel", "parallel")),
        name="ret_sample",
    )(z, z, z, z, *tables, ret_w, ret_b, state, *carried)


def _conv_taps(cb, cc, ch, w_ref, prev1, prev2, seq_len):
    u = cc * ch
    t = lax.broadcasted_iota(jnp.int32, u.shape, 0) & (seq_len - 1)
    um1 = jnp.where(t >= 1, pltpu.roll(u, 1, axis=0), prev1)
    um2 = jnp.where(t >= 2, pltpu.roll(u, 2, axis=0), prev2)
    conv = w_ref[0:1, :] * um2 + w_ref[1:2, :] * um1 + w_ref[2:3, :] * u
    return (cb * conv).astype(BF16), u


def _conv_prompt_kernel(cb_ref, cc_ref, ch_ref, w_ref, o_ref, s_ref, *, seq_len):
    o, u = _conv_taps(cb_ref[...], cc_ref[...], ch_ref[...], w_ref, 0.0, 0.0, seq_len)
    o_ref[...] = o
    s_ref[...] = u[seq_len - (CONV_W - 1):, :]


def _conv_sample_kernel(cb_ref, cc_ref, ch_ref, w_ref, p1_ref, p2_ref, o_any, o_ref, u_ref, *, seq_len):
    del o_any
    o, u = _conv_taps(cb_ref[...], cc_ref[...], ch_ref[...], w_ref, p1_ref[...], p2_ref[...], seq_len)
    o_ref[...] = o
    u_ref[...] = u


def _conv_specs(rows_blk, cols_blk, r0, c0, nb, layer):
    return [
        pl.BlockSpec((rows_blk, cols_blk), lambda r, c: (r0 + r, c0 + c)),
        pl.BlockSpec((rows_blk, cols_blk), lambda r, c: (r0 + r, c0 + nb + c)),
        pl.BlockSpec((rows_blk, cols_blk), lambda r, c: (r0 + r, c0 + 2 * nb + c)),
        pl.BlockSpec((None, CONV_W, cols_blk), lambda r, c: (layer, 0, c)),
    ]


def _conv_prompt(z, conv_w, layer, n_seq, seq_len, col0, cdim, cols_blk):
    m = z.shape[0]
    nb = cdim // cols_blk
    kern = functools.partial(_conv_prompt_kernel, seq_len=seq_len)
    return pl.pallas_call(
        kern,
        grid=(n_seq, nb),
        in_specs=_conv_specs(seq_len, cols_blk, 0, col0 // cols_blk, nb, layer),
        out_specs=[pl.BlockSpec((seq_len, cols_blk), lambda r, c: (r, c)),
                   pl.BlockSpec((None, CONV_W - 1, cols_blk), lambda r, c: (r, 0, c))],
        out_shape=[jax.ShapeDtypeStruct((m, cdim), BF16),
                   jax.ShapeDtypeStruct((n_seq, CONV_W - 1, cdim), F32)],
        compiler_params=_params(("parallel", "parallel")),
        name="conv_prompt",
    )(z, z, z, conv_w)


def _conv_sample(z, conv_w, prev1, prev2, o_full, layer, row0, n_rows, seq_len, col0, cdim, cols_blk):
    nb = cdim // cols_blk
    r0 = row0 // n_rows
    kern = functools.partial(_conv_sample_kernel, seq_len=seq_len)
    return pl.pallas_call(
        kern,
        grid=(1, nb),
        in_specs=_conv_specs(n_rows, cols_blk, r0, col0 // cols_blk, nb, layer) + [
            pl.BlockSpec((n_rows, cols_blk), lambda r, c: (0, c)),
            pl.BlockSpec((n_rows, cols_blk), lambda r, c: (0, c)),
            pl.BlockSpec(memory_space=pl.ANY),
        ],
        out_specs=[pl.BlockSpec((n_rows, cols_blk), lambda r, c: (r0, c)),
                   pl.BlockSpec((n_rows, cols_blk), lambda r, c: (0, c))],
        out_shape=[jax.ShapeDtypeStruct(o_full.shape, BF16),
                   jax.ShapeDtypeStruct((n_rows, cdim), F32)],
        input_output_aliases={6: 0},
        compiler_params=_params(("parallel", "parallel")),
        name="conv_sample",
    )(z, z, z, conv_w, prev1, prev2, o_full)


def _merge_kernel(oa_ref, ob_ref, oc_ref, ga_ref, gb_ref, gc_ref, wa_ref, wb_ref, wc_ref, m_ref):
    m = jax.nn.sigmoid(ga_ref[...]) * _dot(oa_ref[...], wa_ref[...])
    m = m + jax.nn.sigmoid(gb_ref[...]) * _dot(ob_ref[...], wb_ref[...])
    m = m + jax.nn.sigmoid(gc_ref[...]) * _dot(oc_ref[...], wc_ref[...])
    m_ref[...] = m.astype(BF16)


def _merge(oa, ob, oc, z, wa, wb, wc, layer, gate_col0, d):
    m, kdim = oa.shape
    tn = TN_MERGE
    g0 = gate_col0 // tn
    nd = d // tn
    o_spec = pl.BlockSpec((TM, kdim), lambda i, j: (i, 0), pipeline_mode=SINGLE)
    w_spec = pl.BlockSpec((None, kdim, tn), lambda i, j: (layer, 0, j))
    return pl.pallas_call(
        _merge_kernel,
        grid=(m // TM, nd),
        in_specs=[o_spec, o_spec, o_spec,
                  pl.BlockSpec((TM, tn), lambda i, j: (i, g0 + j)),
                  pl.BlockSpec((TM, tn), lambda i, j: (i, g0 + nd + j)),
                  pl.BlockSpec((TM, tn), lambda i, j: (i, g0 + 2 * nd + j)),
                  w_spec, w_spec, w_spec],
        out_specs=pl.BlockSpec((TM, tn), lambda i, j: (i, j)),
        out_shape=jax.ShapeDtypeStruct((m, d), BF16),
        compiler_params=_params(("parallel", "arbitrary")),
        name="merge",
    )(oa, ob, oc, z, z, z, wa, wb, wc)


def _outproj_kernel(m_ref, w_ref, x_ref, o_ref):
    o_ref[...] = x_ref[...] + _dot(m_ref[...], w_ref[...])


def _outproj2_kernel(m_ref, w_ref, xp_ref, xs_ref, o_ref, *, n_first):
    i = pl.program_id(0)
    upd = _dot(m_ref[...], w_ref[...])

    @pl.when(i < pl.num_programs(0) - 1)
    def _():
        o_ref[...] = xp_ref[...] + upd

    @pl.when(i == pl.num_programs(0) - 1)
    def _():
        o_ref[:n_first, :] = xp_ref[:n_first, :] + upd[:n_first]
        o_ref[n_first:, :] = xs_ref[...] + upd[n_first:]


def _outproj(mm, w_out, xs, layer):
    m, d = mm.shape
    if len(xs) == 1:
        kern = _outproj_kernel
        x_specs = [pl.BlockSpec((TM, TN), lambda i, j: (i, j))]
    else:
        ms = xs[1].shape[0]
        kern = functools.partial(_outproj2_kernel, n_first=TM - ms)
        x_specs = [pl.BlockSpec((TM, TN), lambda i, j: (i, j)),
                   pl.BlockSpec((ms, TN), lambda i, j: (0, j))]
    return pl.pallas_call(
        kern,
        grid=(m // TM, d // TN),
        in_specs=[pl.BlockSpec((TM, d), lambda i, j: (i, 0), pipeline_mode=SINGLE),
                  pl.BlockSpec((None, d, TN), lambda i, j: (layer, 0, j))] + x_specs,
        out_specs=pl.BlockSpec((TM, TN), lambda i, j: (i, j)),
        out_shape=jax.ShapeDtypeStruct((m, d), F32),
        compiler_params=_params(("parallel", "arbitrary")),
        name="outproj",
    )(mm, w_out, *xs)


def _mlp_step(x_ref, g_ref, wu_ref, wd_ref, acc_ref, h_scr):
    f = pl.program_id(1)

    @pl.when(f == 0)
    def _():
        x = x_ref[...]
        h_scr[...] = _rmsnorm(x, g_ref[...]).astype(BF16)
        acc_ref[...] = x

    wu = wu_ref[...].astype(BF16)
    wd = wd_ref[...].astype(BF16)
    slab = acc_ref.shape[0] // MLP_SLABS
    for r in range(MLP_SLABS):
        rows = pl.ds(r * slab, slab)
        a = jnp.maximum(jnp.dot(h_scr[rows, :], wu, preferred_element_type=F32), 0.0)
        acc_ref[rows, :] += jnp.dot((a * a).astype(BF16), wd, preferred_element_type=F32)


def _mlp_kernel(x_ref, g_ref, wu_ref, wd_ref, o_ref, h_scr):
    _mlp_step(x_ref, g_ref, wu_ref, wd_ref, o_ref, h_scr)


def _mlp_final_kernel(x_ref, g_ref, wu_ref, wd_ref, gf_ref, yp_ref, ys_ref, h_scr, *, n_first):
    _mlp_step(x_ref, g_ref, wu_ref, wd_ref, yp_ref, h_scr)

    @pl.when(pl.program_id(1) == pl.num_programs(1) - 1)
    def _():
        y = _rmsnorm(yp_ref[...], gf_ref[...])
        yp_ref[...] = y

        @pl.when(pl.program_id(0) == pl.num_programs(0) - 1)
        def _():
            ys_ref[...] = y[n_first:]


def _mlp(x, norm_g, w_up, w_down, layer, tf, final=None):
    m, d = x.shape
    dff = w_up.shape[2]
    tm = TM_MLP
    in_specs = [pl.BlockSpec((tm, d), lambda i, f: (i, 0), pipeline_mode=SINGLE),
                pl.BlockSpec((None, 1, d), lambda i, f: (layer, 0, 0)),
                pl.BlockSpec((None, d, tf), lambda i, f: (layer, 0, f)),
                pl.BlockSpec((None, tf, d), lambda i, f: (layer, f, 0))]
    common = dict(grid=(m // tm, dff // tf), scratch_shapes=[pltpu.VMEM((tm, d), BF16)],
                  compiler_params=_params(("parallel", "arbitrary")))
    if final is None:
        return pl.pallas_call(
            _mlp_kernel, in_specs=in_specs,
            out_specs=pl.BlockSpec((tm, d), lambda i, f: (i, 0)),
            out_shape=jax.ShapeDtypeStruct((m, d), F32), name="mlp", **common,
        )(x, norm_g, w_up, w_down)
    g_final, mp, ms = final
    kern = functools.partial(_mlp_final_kernel, n_first=tm - ms)
    return pl.pallas_call(
        kern, in_specs=in_specs + [pl.BlockSpec((1, d), lambda i, f: (0, 0))],
        out_specs=[pl.BlockSpec((tm, d), lambda i, f: (i, 0)),
                   pl.BlockSpec((ms, d), lambda i, f: (0, 0))],
        out_shape=[jax.ShapeDtypeStruct((mp, d), F32), jax.ShapeDtypeStruct((ms, d), F32)],
        name="mlp_final", **common,
    )(x, norm_g, w_up, w_down, g_final)


def kernel(x_prompt, x_sample, state_gla, state_ret, state_conv, norm_mix, w_in, w_gate2, b_gate2, gla_norm, ret_norm_w, ret_norm_b, conv_w, w_branch_gla, w_branch_ret, w_branch_conv, w_out, norm_mlp, w_up, w_down, norm_final):
    bp, tp, d = x_prompt.shape
    bs, ts, _ = x_sample.shape
    depth = w_in.shape[0]
    dk, dv = state_gla.shape[3], state_gla.shape[4]
    cdim = state_conv.shape[3]
    mp, ms = bp * tp, bs * ts
    rank_col = 2 * GLA_H * dk + GLA_H * dv
    n_cols = w_in.shape[2] - GLA_RANK
    ret_col0 = rank_col + GLA_H * dv
    conv_col0 = ret_col0 + 3 * RET_H * LANE
    gate_col0 = conv_col0 + 3 * cdim

    r3 = lambda a: a.reshape(depth, 1, a.shape[-1])
    norm_mix3, norm_mlp3, b_gate23 = r3(norm_mix), r3(norm_mlp), r3(b_gate2)
    gla_norm3, ret_w3, ret_b3 = r3(gla_norm), r3(ret_norm_w), r3(ret_norm_b)
    w_gate2_pad = jnp.pad(w_gate2, ((0, 0), (0, LANE - GLA_RANK), (0, 0)))
    w_in_t = jnp.transpose(w_in, (0, 2, 1))

    xs = [x_prompt.reshape(mp, d), x_sample.reshape(ms, d)]
    new_gla_p, new_ret_p, new_conv_p, new_conv_s = [], [], [], []
    new_gla_s = new_ret_s = None
    for l in range(depth):
        h, loga = _prenorm(xs, norm_mix3, w_in_t, w_gate2_pad, b_gate23, l, rank_col)
        z = _inproj(h, w_in_t, l, n_cols, rank_col)

        oa, sg_p = _gla_prompt(z, loga, gla_norm3, l, bp, tp, dk, dv, tb=2 * LANE)
        oa, new_gla_s = _gla_sample(z, loga, gla_norm3, state_gla, oa, new_gla_s, l, mp, bs, ts, dk, dv)
        ob, sr_p = _ret_prompt(z, ret_w3, ret_b3, l, bp, tp, ret_col0, tb=LANE)
        ob, new_ret_s = _ret_sample(z, ret_w3, ret_b3, state_ret, ob, new_ret_s, l, mp, bs, ts, ret_col0)

        oc, sc_p = _conv_prompt(z, conv_w, l, bp, tp, conv_col0, cdim, 2 * LANE)
        sc = state_conv[l]
        pad_rows = jnp.zeros((bs, ts - 1, cdim), F32)
        prev1 = jnp.concatenate([sc[:, 1:2], pad_rows], axis=1).reshape(ms, cdim)
        prev2 = jnp.concatenate([sc, pad_rows[:, 1:]], axis=1).reshape(ms, cdim)
        oc, u_s = _conv_sample(z, conv_w, prev1, prev2, oc, l, mp, ms, ts, conv_col0, cdim, 2 * LANE)

        mm = _merge(oa, ob, oc, z, w_branch_gla, w_branch_ret, w_branch_conv, l, gate_col0, d)
        x = _outproj(mm, w_out, xs, l)
        if l + 1 < depth:
            xs = [_mlp(x, norm_mlp3, w_up, w_down, l, tf=TN)]
        else:
            y_p, y_s = _mlp(x, norm_mlp3, w_up, w_down, l, tf=TN, final=(norm_final.reshape(1, d), mp, ms))

        new_gla_p.append(sg_p)
        new_ret_p.append(sr_p)
        new_conv_p.append(sc_p)
        new_conv_s.append(u_s.reshape(bs, ts, cdim)[:, ts - (CONV_W - 1):])

    return (y_p.reshape(bp, tp, d), y_s.reshape(bs, ts, d),
            jnp.stack(new_gla_p), jnp.stack(new_ret_p), jnp.stack(new_conv_p),
            new_gla_s, new_ret_s, jnp.stack(new_conv_s))
```

```python
import functools
import math

import numpy as np
import jax
import jax.numpy as jnp
from jax import lax
from jax.experimental import pallas as pl
from jax.experimental.pallas import tpu as pltpu

F32 = jnp.float32
BF16 = jnp.bfloat16

GLA_H = 4
GLA_RANK = 16
GATE_TAU = 16.0
RET_H = 8
ROPE_BASE = 10000.0
CONV_W = 3
EPS = 1e-6
GN_EPS = 1e-5
PAST_LEN = 16384

LANE = 128
VMEM_LIMIT = 60 * 1024 * 1024
TM = 2176
TM_MLP = 1088
TN = 512
TN_MERGE = 256
TR = 512
MLP_SLABS = 2
SINGLE = pl.Buffered(1)


def _dot(a, b):
    return jnp.dot(a.astype(BF16), b.astype(BF16), preferred_element_type=F32)


def _dot_nt(a, b):
    return lax.dot_general(a.astype(BF16), b.astype(BF16), (((1,), (1,)), ((), ())),
                           preferred_element_type=F32)


def _rmsnorm(x, g):
    y = x * lax.rsqrt(jnp.mean(x * x, axis=-1, keepdims=True) + EPS)
    return y * g


def _silu(x):
    return x * jax.nn.sigmoid(x)


def _params(sem):
    return pltpu.CompilerParams(dimension_semantics=sem, vmem_limit_bytes=VMEM_LIMIT)


def _prenorm_body(x, g_ref, wrank_ref, wg2_ref, bg2_ref, h_ref, loga_ref):
    h = _rmsnorm(x, g_ref[...]).astype(BF16)
    h_ref[...] = h
    alr = _dot_nt(h, wrank_ref[...])
    pre = _dot(alr, wg2_ref[...]) + bg2_ref[...]
    loga_ref[...] = (jnp.minimum(pre, 0.0) - jnp.log1p(jnp.exp(-jnp.abs(pre)))) * (1.0 / GATE_TAU)


def _prenorm_kernel(x_ref, g_ref, wrank_ref, wg2_ref, bg2_ref, h_ref, loga_ref):
    _prenorm_body(x_ref[...], g_ref, wrank_ref, wg2_ref, bg2_ref, h_ref, loga_ref)


def _prenorm2_kernel(xp_ref, xs_ref, g_ref, wrank_ref, wg2_ref, bg2_ref, h_ref, loga_ref, *, n_first):
    i = pl.program_id(0)

    @pl.when(i < n_first)
    def _():
        _prenorm_body(xp_ref[...], g_ref, wrank_ref, wg2_ref, bg2_ref, h_ref, loga_ref)

    @pl.when(i >= n_first)
    def _():
        _prenorm_body(xs_ref[...], g_ref, wrank_ref, wg2_ref, bg2_ref, h_ref, loga_ref)


def _prenorm(xs, norm_g, w_in, w_gate2_pad, b_gate2, layer, rank_col):
    d = xs[0].shape[1]
    m = sum(x.shape[0] for x in xs)
    gk = w_gate2_pad.shape[2]
    if len(xs) == 1:
        kern = _prenorm_kernel
        x_specs = [pl.BlockSpec((TR, d), lambda i: (i, 0))]
    else:
        n_first = xs[0].shape[0] // TR
        kern = functools.partial(_prenorm2_kernel, n_first=n_first)
        x_specs = [pl.BlockSpec((TR, d), lambda i: (jnp.minimum(i, n_first - 1), 0)),
                   pl.BlockSpec((TR, d), lambda i: (jnp.maximum(i - n_first, 0), 0))]
    return pl.pallas_call(
        kern,
        grid=(m // TR,),
        in_specs=x_specs + [
            pl.BlockSpec((None, 1, d), lambda i: (layer, 0, 0)),
            pl.BlockSpec((None, LANE, d), lambda i: (layer, rank_col // LANE, 0)),
            pl.BlockSpec((None, LANE, gk), lambda i: (layer, 0, 0)),
            pl.BlockSpec((None, 1, gk), lambda i: (layer, 0, 0)),
        ],
        out_specs=[pl.BlockSpec((TR, d), lambda i: (i, 0)),
                   pl.BlockSpec((TR, gk), lambda i: (i, 0))],
        out_shape=[jax.ShapeDtypeStruct((m, d), BF16),
                   jax.ShapeDtypeStruct((m, gk), F32)],
        compiler_params=_params(("parallel",)),
        name="prenorm",
    )(*xs, norm_g, w_in, w_gate2_pad, b_gate2)


def _inproj_kernel(h_ref, wmain_ref, wext_ref, z_ref, *, n_plain, shift):
    j = pl.program_id(1)

    @pl.when(j < n_plain)
    def _():
        z_ref[...] = _dot_nt(h_ref[...], wmain_ref[...])

    @pl.when(j >= n_plain)
    def _():
        tn = wmain_ref.shape[0]
        w = jnp.concatenate([wmain_ref[...].astype(BF16), wext_ref[...].astype(BF16)], axis=0)
        z_ref[...] = _dot_nt(h_ref[...], w[shift:shift + tn, :])


def _inproj(h, w_in, layer, n_cols, rank_col):
    m, d = h.shape
    ext_per_tile = TN // LANE
    kern = functools.partial(_inproj_kernel, n_plain=rank_col // TN, shift=GLA_RANK)
    return pl.pallas_call(
        kern,
        grid=(m // TM, n_cols // TN),
        in_specs=[
            pl.BlockSpec((TM, d), lambda i, j: (i, 0)),
            pl.BlockSpec((None, TN, d), lambda i, j: (layer, j, 0)),
            pl.BlockSpec((None, LANE, d), lambda i, j: (layer, ext_per_tile * (j + 1), 0)),
        ],
        out_specs=pl.BlockSpec((TM, TN), lambda i, j: (i, j)),
        out_shape=jax.ShapeDtypeStruct((m, n_cols), F32),
        compiler_params=_params(("parallel", "arbitrary")),
        name="inproj",
    )(h, w_in, w_in)


def _block_sums(g, top):
    row = lax.broadcasted_iota(jnp.int32, g.shape, 0)
    n = g.shape[0]
    c, d, tot = g, jnp.zeros_like(g), g
    out = {1: (c, d)}
    m = 1
    while m < top:
        later = (row & m) != 0
        tot_prev = pltpu.roll(tot, m, axis=0)
        tot_next = pltpu.roll(tot, n - m, axis=0)
        c = c + jnp.where(later, tot_prev, 0.0)
        d = d + jnp.where(later, 0.0, tot_next)
        tot = tot + jnp.where(later, tot_prev, tot_next)
        m *= 2
        out[m] = (c, d)
    return out, tot


def _gla_intra(qe, k, v, sums, seq_len):
    t = lax.broadcasted_iota(jnp.int32, (LANE, LANE), 0)
    s = lax.broadcasted_iota(jnp.int32, (LANE, LANE), 1)
    a = jnp.where(t == s, _dot_nt(qe, k), 0.0)
    m = 1
    while m < seq_len:
        c, d = sums[m]
        km = k if m == 1 else k * jnp.exp(d)
        sh = int(math.log2(2 * m))
        mask = ((t >> sh) == (s >> sh)) & ((t & m) != 0) & ((s & m) == 0)
        a = jnp.where(mask, _dot_nt(qe * jnp.exp(c), km), a)
        m *= 2
    return _dot(a, v)


def _gla_finish(o, ra, gn):
    y = o * lax.rsqrt(jnp.mean(o * o, axis=-1, keepdims=True) + EPS)
    return ((y * gn) * _silu(ra)).astype(BF16)


def _gla_prompt_kernel(q_ref, k_ref, v_ref, ra_ref, g_ref, gn_ref, o_ref, s_ref, st_scr,
                       *, scale, n_chunks, dk, dv):
    tb = pl.program_id(1)

    @pl.when(tb == 0)
    def _():
        st_scr[...] = jnp.zeros_like(st_scr)

    for c in range(n_chunks):
        rows = pl.ds(c * LANE, LANE)
        sums_all, tot_all = _block_sums(g_ref[rows, :], LANE)
        for h in range(GLA_H):
            kl = slice(h * dk, (h + 1) * dk)
            vl = slice(h * dv, (h + 1) * dv)
            sums = {m: (cd[0][:, kl], cd[1][:, kl]) for m, cd in sums_all.items()}
            qe = q_ref[rows, kl] * scale
            k = k_ref[rows, kl]
            v = v_ref[rows, vl]
            c_seq, d_seq = sums[LANE]
            st = st_scr[h]
            o = _gla_intra(qe, k, v, sums, LANE) + _dot_nt(qe * jnp.exp(c_seq), st)
            st_scr[h] = st * jnp.exp(tot_all[0:1, kl]) + _dot(v.T, k * jnp.exp(d_seq))
            o_ref[rows, vl] = _gla_finish(o, ra_ref[rows, vl], gn_ref[:, vl])

    @pl.when(tb == pl.num_programs(1) - 1)
    def _():
        for h in range(GLA_H):
            s_ref[h] = st_scr[h].T


def _gla_prompt(z, loga, gla_norm, layer, n_seq, seq_len, dk, dv, tb):
    m = z.shape[0]
    nt = seq_len // tb
    hk, hv = GLA_H * dk, GLA_H * dv
    kern = functools.partial(_gla_prompt_kernel, scale=dk ** -0.5, n_chunks=tb // LANE, dk=dk, dv=dv)
    row = lambda b, t: b * nt + t
    return pl.pallas_call(
        kern,
        grid=(n_seq, nt),
        in_specs=[
            pl.BlockSpec((tb, hk), lambda b, t: (row(b, t), 0)),
            pl.BlockSpec((tb, hk), lambda b, t: (row(b, t), 1)),
            pl.BlockSpec((tb, hv), lambda b, t: (row(b, t), (2 * hk) // hv)),
            pl.BlockSpec((tb, hv), lambda b, t: (row(b, t), (2 * hk + hv) // hv)),
            pl.BlockSpec((tb, hk), lambda b, t: (row(b, t), 0)),
            pl.BlockSpec((None, 1, hv), lambda b, t: (layer, 0, 0)),
        ],
        out_specs=[
            pl.BlockSpec((tb, hv), lambda b, t: (row(b, t), 0)),
            pl.BlockSpec((None, GLA_H, dk, dv), lambda b, t: (b, 0, 0, 0)),
        ],
        out_shape=[jax.ShapeDtypeStruct((m, hv), BF16),
                   jax.ShapeDtypeStruct((n_seq, GLA_H, dk, dv), F32)],
        scratch_shapes=[pltpu.VMEM((GLA_H, dv, dk), F32)],
        compiler_params=_params(("parallel", "arbitrary")),
        name="gla_prompt",
    )(z, z, z, z, loga, gla_norm)


def _gla_sample_kernel(*refs, scale, seq_len):
    q_ref, k_ref, v_ref, ra_ref, g_ref, gn_ref, s_in_ref = refs[:7]
    o_ref, s_out_ref = refs[-2:]
    n = LANE // seq_len
    sh = int(math.log2(seq_len))
    sums, tot = _block_sums(g_ref[...], seq_len)
    c_seq, d_seq = sums[seq_len]
    qe = q_ref[...] * scale
    k = k_ref[...]
    v = v_ref[...]
    dk, dv = k.shape[1], v.shape[1]
    o = _gla_intra(qe, k, v, sums, seq_len)

    q_b = (qe * jnp.exp(c_seq)).astype(BF16)
    row = lax.broadcasted_iota(jnp.int32, q_b.shape, 0)
    lhs = jnp.concatenate([jnp.where((row >> sh) == b, q_b, jnp.zeros_like(q_b)) for b in range(n)], axis=1)
    s_all = s_in_ref[...]
    o = o + jnp.dot(lhs, s_all.reshape(n * dk, dv).astype(BF16), preferred_element_type=F32)
    o_ref[...] = _gla_finish(o, ra_ref[...], gn_ref[...])

    kd_t = (k * jnp.exp(d_seq)).T.astype(BF16)
    dec_t = jnp.exp(tot).T
    seq = lax.broadcasted_iota(jnp.int32, (n, dk, LANE), 0)
    col = lax.broadcasted_iota(jnp.int32, (n, dk, LANE), 2)
    kd_blk = jnp.where((col >> sh) == seq, kd_t[None], jnp.zeros((), BF16))
    upd = jnp.dot(kd_blk.reshape(n * dk, LANE), v.astype(BF16), preferred_element_type=F32)
    dcol = jnp.sum(jnp.where(col == seq * seq_len, dec_t[None], 0.0), axis=2, keepdims=True)
    s_out_ref[...] = dcol * s_all + upd.reshape(n, dk, dv)


def _gla_sample(z, loga, gla_norm, state, o_full, s_full, layer, row0, n_seq, seq_len, dk, dv):
    depth = state.shape[0]
    per_tile = LANE // seq_len
    r0 = row0 // LANE
    kern = functools.partial(_gla_sample_kernel, scale=dk ** -0.5, seq_len=seq_len)
    v_blk0 = (2 * GLA_H * dk) // dv
    ra_blk0 = (2 * GLA_H * dk + GLA_H * dv) // dv
    carried = [o_full] if s_full is None else [o_full, s_full]
    return pl.pallas_call(
        kern,
        grid=(n_seq // per_tile, GLA_H),
        in_specs=[
            pl.BlockSpec((LANE, dk), lambda i, h: (r0 + i, h)),
            pl.BlockSpec((LANE, dk), lambda i, h: (r0 + i, GLA_H + h)),
            pl.BlockSpec((LANE, dv), lambda i, h: (r0 + i, v_blk0 + h)),
            pl.BlockSpec((LANE, dv), lambda i, h: (r0 + i, ra_blk0 + h)),
            pl.BlockSpec((LANE, dk), lambda i, h: (r0 + i, h)),
            pl.BlockSpec((None, 1, dv), lambda i, h: (layer, 0, h)),
            pl.BlockSpec((None, per_tile, None, dk, dv), lambda i, h: (layer, i, h, 0, 0)),
        ] + [pl.BlockSpec(memory_space=pl.ANY)] * len(carried),
        out_specs=[
            pl.BlockSpec((LANE, dv), lambda i, h: (r0 + i, h)),
            pl.BlockSpec((None, per_tile, None, dk, dv), lambda i, h: (layer, i, h, 0, 0)),
        ],
        out_shape=[jax.ShapeDtypeStruct(o_full.shape, BF16),
                   jax.ShapeDtypeStruct((depth, n_seq, GLA_H, dk, dv), F32)],
        input_output_aliases={7 + n: n for n in range(len(carried))},
        compiler_params=_params(("parallel", "parallel")),
        name="gla_sample",
    )(z, z, z, z, loga, gla_norm, state, *carried)


def _ret_tables(seq_len, pos0, n_rows):
    dk = LANE // 2
    half = dk // 2
    log_gamma = jnp.log1p(-jnp.exp2(-5.0 - jnp.arange(RET_H, dtype=F32)))
    pos = (pos0 + (jnp.arange(n_rows) % seq_len)).astype(F32)
    inv_freq = ROPE_BASE ** (-jnp.arange(half, dtype=F32) / half)
    ang = pos[:, None] * inv_freq[None, :]
    cos, sin = jnp.cos(ang), jnp.sin(ang)
    cos_t = jnp.tile(cos, (1, 4))
    sin_t = jnp.tile(jnp.concatenate([-sin, sin], axis=1), (1, 2))
    r = jnp.arange(LANE)
    idx = (r % seq_len).astype(F32)
    diff = idx[:, None] - idx[None, :]
    causal = (diff >= 0) & ((r[:, None] // seq_len) == (r[None, :] // seq_len))
    dm = jnp.where(causal[None], jnp.exp(jnp.where(causal, diff, 0.0)[None] * log_gamma[:, None, None]), 0.0)
    xi = jnp.exp((idx[:, None] + 1.0) * log_gamma[None, :])
    zeta = jnp.exp((seq_len - 1.0 - idx[:, None]) * log_gamma[None, :])
    g_l = jnp.exp(seq_len * log_gamma)
    xi_t = jnp.broadcast_to(xi.T[:, :, None], (RET_H, LANE, LANE))
    zeta_t = jnp.repeat(zeta.reshape(LANE, RET_H // 2, 2), dk, axis=2).transpose(1, 0, 2)
    gl_t = jnp.broadcast_to(jnp.repeat(g_l.reshape(RET_H // 2, 2), dk, axis=1)[:, :, None],
                            (RET_H // 2, LANE, LANE))
    return cos_t, sin_t, dm, xi_t, zeta_t, gl_t


def _ret_tile(q, k, v, cos, sin, dm0, dm1, scale):
    lane = lax.broadcasted_iota(jnp.int32, q.shape, 1)
    first = (lane & (LANE // 4)) == 0

    def rot(x):
        sw = jnp.where(first, pltpu.roll(x, LANE - LANE // 4, axis=1), pltpu.roll(x, LANE // 4, axis=1))
        return x * cos + sw * sin

    q = rot(q) * scale
    k = rot(k)
    lo = lane < LANE // 2
    q0 = jnp.where(lo, q, 0.0)
    q1 = jnp.where(lo, 0.0, q)
    a0 = _dot_nt(q0, k) * dm0
    a1 = _dot_nt(q1, k) * dm1
    o0 = _dot(a0, v[:, :LANE])
    o1 = _dot(a1, v[:, LANE:])
    return k, q0, q1, o0, o1


def _ret_finish(o, gr, w, b):
    mu = jnp.mean(o, axis=-1, keepdims=True)
    c = o - mu
    y = c * lax.rsqrt(jnp.mean(c * c, axis=-1, keepdims=True) + GN_EPS)
    return ((y * w + b) * _silu(gr)).astype(BF16)


def _ret_prompt_kernel(q_ref, k_ref, v_ref, gr_ref, cos_ref, sin_ref, dm_ref, xi_ref, zeta_ref, gl_ref,
                       w_ref, b_ref, o_ref, s_ref, p_scr, *, scale, n_chunks):
    tb = pl.program_id(1)
    pairs = RET_H // 2
    half = LANE // 2

    @pl.when(tb == 0)
    def _():
        p_scr[...] = jnp.zeros_like(p_scr)

    for c in range(n_chunks):
        rows = pl.ds(c * LANE, LANE)
        cos, sin = cos_ref[rows, :], sin_ref[rows, :]
        for j in range(pairs):
            kl = slice(j * LANE, (j + 1) * LANE)
            v0l = slice(2 * j * LANE, (2 * j + 1) * LANE)
            v1l = slice((2 * j + 1) * LANE, (2 * j + 2) * LANE)
            vl = slice(2 * j * LANE, (2 * j + 2) * LANE)
            v = v_ref[rows, vl]
            k, q0, q1, o0, o1 = _ret_tile(q_ref[rows, kl], k_ref[rows, kl], v, cos, sin,
                                          dm_ref[2 * j], dm_ref[2 * j + 1], scale)
            p = p_scr[j]
            o0 = o0 + _dot(q0, p[:, :LANE]) * xi_ref[2 * j]
            o1 = o1 + _dot(q1, p[:, LANE:]) * xi_ref[2 * j + 1]
            gl = gl_ref[j]
            p_scr[j] = jnp.concatenate([gl, gl], axis=1) * p + _dot((k * zeta_ref[j]).T, v)
            o_ref[rows, v0l] = _ret_finish(o0, gr_ref[rows, v0l], w_ref[:, v0l], b_ref[:, v0l])
            o_ref[rows, v1l] = _ret_finish(o1, gr_ref[rows, v1l], w_ref[:, v1l], b_ref[:, v1l])

    @pl.when(tb == pl.num_programs(1) - 1)
    def _():
        for j in range(pairs):
            s_ref[2 * j] = p_scr[j, :half, :LANE]
            s_ref[2 * j + 1] = p_scr[j, half:, LANE:]


def _ret_prompt(z, ret_w, ret_b, layer, n_seq, seq_len, col0, tb):
    m = z.shape[0]
    nt = seq_len // tb
    pairs = RET_H // 2
    hk, hv = RET_H * LANE // 2, RET_H * LANE
    _, _, dm, xi_t, zeta_t, gl_t = _ret_tables(LANE, 0, LANE)
    cos_t, sin_t = _ret_tables(seq_len, 0, seq_len)[:2]
    kern = functools.partial(_ret_prompt_kernel, scale=(LANE // 2) ** -0.5, n_chunks=tb // LANE)
    row = lambda b, t: b * nt + t
    full = lambda a: pl.BlockSpec(a.shape, lambda b, t: (0,) * a.ndim)
    return pl.pallas_call(
        kern,
        grid=(n_seq, nt),
        in_specs=[
            pl.BlockSpec((tb, hk), lambda b, t: (row(b, t), col0 // hk)),
            pl.BlockSpec((tb, hk), lambda b, t: (row(b, t), (col0 + hk) // hk)),
            pl.BlockSpec((tb, hv), lambda b, t: (row(b, t), (col0 + 2 * hk) // hv)),
            pl.BlockSpec((tb, hv), lambda b, t: (row(b, t), (col0 + 2 * hk + hv) // hv)),
            pl.BlockSpec((tb, LANE), lambda b, t: (t, 0)),
            pl.BlockSpec((tb, LANE), lambda b, t: (t, 0)),
            full(dm), full(xi_t), full(zeta_t), full(gl_t),
            pl.BlockSpec((None, 1, hv), lambda b, t: (layer, 0, 0)),
            pl.BlockSpec((None, 1, hv), lambda b, t: (layer, 0, 0)),
        ],
        out_specs=[
            pl.BlockSpec((tb, hv), lambda b, t: (row(b, t), 0)),
            pl.BlockSpec((None, RET_H, LANE // 2, LANE), lambda b, t: (b, 0, 0, 0)),
        ],
        out_shape=[jax.ShapeDtypeStruct((m, hv), BF16),
                   jax.ShapeDtypeStruct((n_seq, RET_H, LANE // 2, LANE), F32)],
        scratch_shapes=[pltpu.VMEM((pairs, LANE, 2 * LANE), F32)],
        compiler_params=_params(("parallel", "arbitrary")),
        name="ret_prompt",
    )(z, z, z, z, cos_t, sin_t, dm, xi_t, zeta_t, gl_t, ret_w, ret_b)


def _ret_sample_kernel(*refs, scale, seq_len):
    (q_ref, k_ref, v_ref, gr_ref, cos_ref, sin_ref, dm_ref, xi_ref, zeta_ref, gl_ref,
     w_ref, b_ref, s_in_ref) = refs[:13]
    o_ref, s_out_ref = refs[-2:]
    n = LANE // seq_len
    sh = int(math.log2(seq_len))
    half = LANE // 2
    v = v_ref[...]
    k, q0, q1, o0, o1 = _ret_tile(q_ref[...], k_ref[...], v, cos_ref[...], sin_ref[...],
                                  dm_ref[0], dm_ref[1], scale)

    p4 = s_in_ref[...]
    p_bf = p4.reshape(n * LANE, LANE).astype(BF16)
    row = lax.broadcasted_iota(jnp.int32, q0.shape, 0)

    def spread(q):
        q_b = q.astype(BF16)
        return jnp.concatenate([jnp.where((row >> sh) == b, q_b, jnp.zeros_like(q_b)) for b in range(n)], axis=1)

    o0 = o0 + jnp.dot(spread(q0), p_bf, preferred_element_type=F32) * xi_ref[0]
    o1 = o1 + jnp.dot(spread(q1), p_bf, preferred_element_type=F32) * xi_ref[1]
    gr = gr_ref[...]
    o_ref[:, :LANE] = _ret_finish(o0, gr[:, :LANE], w_ref[:, :LANE], b_ref[:, :LANE])
    o_ref[:, LANE:] = _ret_finish(o1, gr[:, LANE:], w_ref[:, LANE:], b_ref[:, LANE:])

    kz_t = (k * zeta_ref[...]).T.astype(BF16)
    seq = lax.broadcasted_iota(jnp.int32, (n, LANE, LANE), 0)
    col = lax.broadcasted_iota(jnp.int32, (n, LANE, LANE), 2)
    kz_blk = jnp.where((col >> sh) == seq, kz_t[None], jnp.zeros((), BF16))
    upd = jnp.dot(kz_blk.reshape(n * LANE, LANE), v.astype(BF16), preferred_element_type=F32)
    upd = upd.reshape(n, 2, half, 2 * LANE)
    upd = jnp.concatenate([upd[:, 0:1, :, :LANE], upd[:, 1:2, :, LANE:]], axis=1)
    s_out_ref[...] = gl_ref[...].reshape(1, 2, half, LANE) * p4 + upd


def _ret_sample(z, ret_w, ret_b, state, o_full, s_full, layer, row0, n_seq, seq_len, col0):
    depth = state.shape[0]
    per_tile = LANE // seq_len
    pairs = RET_H // 2
    dk2, dv2 = LANE, 2 * LANE
    r0 = row0 // LANE
    tables = _ret_tables(seq_len, PAST_LEN, LANE)
    kern = functools.partial(_ret_sample_kernel, scale=(LANE // 2) ** -0.5, seq_len=seq_len)
    q0 = col0 // dk2
    k0 = (col0 + RET_H * LANE // 2) // dk2
    v0 = (col0 + RET_H * LANE) // dv2
    g0 = (col0 + 2 * RET_H * LANE) // dv2
    carried = [o_full] if s_full is None else [o_full, s_full]
    return pl.pallas_call(
        kern,
        grid=(n_seq // per_tile, pairs),
        in_specs=[
            pl.BlockSpec((LANE, dk2), lambda i, j: (r0 + i, q0 + j)),
            pl.BlockSpec((LANE, dk2), lambda i, j: (r0 + i, k0 + j)),
            pl.BlockSpec((LANE, dv2), lambda i, j: (r0 + i, v0 + j)),
            pl.BlockSpec((LANE, dv2), lambda i, j: (r0 + i, g0 + j)),
            pl.BlockSpec((LANE, LANE), lambda i, j: (0, 0)),
            pl.BlockSpec((LANE, LANE), lambda i, j: (0, 0)),
            pl.BlockSpec((2, LANE, LANE), lambda i, j: (j, 0, 0)),
            pl.BlockSpec((2, LANE, LANE), lambda i, j: (j, 0, 0)),
            pl.BlockSpec((None, LANE, LANE), lambda i, j: (j, 0, 0)),
            pl.BlockSpec((None, LANE, LANE), lambda i, j: (j, 0, 0)),
            pl.BlockSpec((None, 1, dv2), lambda i, j: (layer, 0, j)),
            pl.BlockSpec((None, 1, dv2), lambda i, j: (layer, 0, j)),
            pl.BlockSpec((None, per_tile, 2, LANE // 2, LANE), lambda i, j: (layer, i, j, 0, 0)),
        ] + [pl.BlockSpec(memory_space=pl.ANY)] * len(carried),
        out_specs=[
            pl.BlockSpec((LANE, dv2), lambda i, j: (r0 + i, j)),
            pl.BlockSpec((None, per_tile, 2, LANE // 2, LANE), lambda i, j: (layer, i, j, 0, 0)),
        ],
        out_shape=[jax.ShapeDtypeStruct(o_full.shape, BF16),
                   jax.ShapeDtypeStruct((depth, n_seq, RET_H, LANE // 2, LANE), F32)],
        input_output_aliases={13 + n: n for n in range(len(carried))},
        compiler_params=_params(("parallel", "parallel")),
        name="ret_sample",
    )(z, z, z, z, *tables, ret_w, ret_b, state, *carried)


def _conv_taps(cb, cc, ch, w_ref, prev1, prev2, seq_len):
    u = cc * ch
    t = lax.broadcasted_iota(jnp.int32, u.shape, 0) & (seq_len - 1)
    um1 = jnp.where(t >= 1, pltpu.roll(u, 1, axis=0), prev1)
    um2 = jnp.where(t >= 2, pltpu.roll(u, 2, axis=0), prev2)
    conv = w_ref[0:1, :] * um2 + w_ref[1:2, :] * um1 + w_ref[2:3, :] * u
    return (cb * conv).astype(BF16), u


def _conv_prompt_kernel(cb_ref, cc_ref, ch_ref, w_ref, o_ref, s_ref, *, seq_len):
    o, u = _conv_taps(cb_ref[...], cc_ref[...], ch_ref[...], w_ref, 0.0, 0.0, seq_len)
    o_ref[...] = o
    s_ref[...] = u[seq_len - (CONV_W - 1):, :]


def _conv_sample_kernel(cb_ref, cc_ref, ch_ref, w_ref, p1_ref, p2_ref, o_any, o_ref, u_ref, *, seq_len):
    del o_any
    o, u = _conv_taps(cb_ref[...], cc_ref[...], ch_ref[...], w_ref, p1_ref[...], p2_ref[...], seq_len)
    o_ref[...] = o
    u_ref[...] = u


def _conv_specs(rows_blk, cols_blk, r0, c0, nb, layer):
    return [
        pl.BlockSpec((rows_blk, cols_blk), lambda r, c: (r0 + r, c0 + c)),
        pl.BlockSpec((rows_blk, cols_blk), lambda r, c: (r0 + r, c0 + nb + c)),
        pl.BlockSpec((rows_blk, cols_blk), lambda r, c: (r0 + r, c0 + 2 * nb + c)),
        pl.BlockSpec((None, CONV_W, cols_blk), lambda r, c: (layer, 0, c)),
    ]


def _conv_prompt(z, conv_w, layer, n_seq, seq_len, col0, cdim, cols_blk):
    m = z.shape[0]
    nb = cdim // cols_blk
    kern = functools.partial(_conv_prompt_kernel, seq_len=seq_len)
    return pl.pallas_call(
        kern,
        grid=(n_seq, nb),
        in_specs=_conv_specs(seq_len, cols_blk, 0, col0 // cols_blk, nb, layer),
        out_specs=[pl.BlockSpec((seq_len, cols_blk), lambda r, c: (r, c)),
                   pl.BlockSpec((None, CONV_W - 1, cols_blk), lambda r, c: (r, 0, c))],
        out_shape=[jax.ShapeDtypeStruct((m, cdim), BF16),
                   jax.ShapeDtypeStruct((n_seq, CONV_W - 1, cdim), F32)],
        compiler_params=_params(("parallel", "parallel")),
        name="conv_prompt",
    )(z, z, z, conv_w)


def _conv_sample(z, conv_w, prev1, prev2, o_full, layer, row0, n_rows, seq_len, col0, cdim, cols_blk):
    nb = cdim // cols_blk
    r0 = row0 // n_rows
    kern = functools.partial(_conv_sample_kernel, seq_len=seq_len)
    return pl.pallas_call(
        kern,
        grid=(1, nb),
        in_specs=_conv_specs(n_rows, cols_blk, r0, col0 // cols_blk, nb, layer) + [
            pl.BlockSpec((n_rows, cols_blk), lambda r, c: (0, c)),
            pl.BlockSpec((n_rows, cols_blk), lambda r, c: (0, c)),
            pl.BlockSpec(memory_space=pl.ANY),
        ],
        out_specs=[pl.BlockSpec((n_rows, cols_blk), lambda r, c: (r0, c)),
                   pl.BlockSpec((n_rows, cols_blk), lambda r, c: (0, c))],
        out_shape=[jax.ShapeDtypeStruct(o_full.shape, BF16),
                   jax.ShapeDtypeStruct((n_rows, cdim), F32)],
        input_output_aliases={6: 0},
        compiler_params=_params(("parallel", "parallel")),
        name="conv_sample",
    )(z, z, z, conv_w, prev1, prev2, o_full)


def _merge_kernel(oa_ref, ob_ref, oc_ref, ga_ref, gb_ref, gc_ref, wa_ref, wb_ref, wc_ref, m_ref):
    m = jax.nn.sigmoid(ga_ref[...]) * _dot(oa_ref[...], wa_ref[...])
    m = m + jax.nn.sigmoid(gb_ref[...]) * _dot(ob_ref[...], wb_ref[...])
    m = m + jax.nn.sigmoid(gc_ref[...]) * _dot(oc_ref[...], wc_ref[...])
    m_ref[...] = m.astype(BF16)


def _merge(oa, ob, oc, z, wa, wb, wc, layer, gate_col0, d):
    m, kdim = oa.shape
    tn = TN_MERGE
    g0 = gate_col0 // tn
    nd = d // tn
    o_spec = pl.BlockSpec((TM, kdim), lambda i, j: (i, 0), pipeline_mode=SINGLE)
    w_spec = pl.BlockSpec((None, kdim, tn), lambda i, j: (layer, 0, j))
    return pl.pallas_call(
        _merge_kernel,
        grid=(m // TM, nd),
        in_specs=[o_spec, o_spec, o_spec,
                  pl.BlockSpec((TM, tn), lambda i, j: (i, g0 + j)),
                  pl.BlockSpec((TM, tn), lambda i, j: (i, g0 + nd + j)),
                  pl.BlockSpec((TM, tn), lambda i, j: (i, g0 + 2 * nd + j)),
                  w_spec, w_spec, w_spec],
        out_specs=pl.BlockSpec((TM, tn), lambda i, j: (i, j)),
        out_shape=jax.ShapeDtypeStruct((m, d), BF16),
        compiler_params=_params(("parallel", "arbitrary")),
        name="merge",
    )(oa, ob, oc, z, z, z, wa, wb, wc)


def _outproj_kernel(m_ref, w_ref, x_ref, o_ref):
    o_ref[...] = x_ref[...] + _dot(m_ref[...], w_ref[...])


def _outproj2_kernel(m_ref, w_ref, xp_ref, xs_ref, o_ref, *, n_first):
    i = pl.program_id(0)
    upd = _dot(m_ref[...], w_ref[...])

    @pl.when(i < pl.num_programs(0) - 1)
    def _():
        o_ref[...] = xp_ref[...] + upd

    @pl.when(i == pl.num_programs(0) - 1)
    def _():
        o_ref[:n_first, :] = xp_ref[:n_first, :] + upd[:n_first]
        o_ref[n_first:, :] = xs_ref[...] + upd[n_first:]


def _outproj(mm, w_out, xs, layer):
    m, d = mm.shape
    if len(xs) == 1:
        kern = _outproj_kernel
        x_specs = [pl.BlockSpec((TM, TN), lambda i, j: (i, j))]
    else:
        ms = xs[1].shape[0]
        kern = functools.partial(_outproj2_kernel, n_first=TM - ms)
        x_specs = [pl.BlockSpec((TM, TN), lambda i, j: (i, j)),
                   pl.BlockSpec((ms, TN), lambda i, j: (0, j))]
    return pl.pallas_call(
        kern,
        grid=(m // TM, d // TN),
        in_specs=[pl.BlockSpec((TM, d), lambda i, j: (i, 0), pipeline_mode=SINGLE),
                  pl.BlockSpec((None, d, TN), lambda i, j: (layer, 0, j))] + x_specs,
        out_specs=pl.BlockSpec((TM, TN), lambda i, j: (i, j)),
        out_shape=jax.ShapeDtypeStruct((m, d), F32),
        compiler_params=_params(("parallel", "arbitrary")),
        name="outproj",
    )(mm, w_out, *xs)


def _mlp_step(x_ref, g_ref, wu_ref, wd_ref, acc_ref, h_scr):
    f = pl.program_id(1)

    @pl.when(f == 0)
    def _():
        x = x_ref[...]
        h_scr[...] = _rmsnorm(x, g_ref[...]).astype(BF16)
        acc_ref[...] = x

    wu = wu_ref[...].astype(BF16)
    wd = wd_ref[...].astype(BF16)
    slab = acc_ref.shape[0] // MLP_SLABS
    for r in range(MLP_SLABS):
        rows = pl.ds(r * slab, slab)
        a = jnp.maximum(jnp.dot(h_scr[rows, :], wu, preferred_element_type=F32), 0.0)
        acc_ref[rows, :] += jnp.dot((a * a).astype(BF16), wd, preferred_element_type=F32)


def _mlp_kernel(x_ref, g_ref, wu_ref, wd_ref, o_ref, h_scr):
    _mlp_step(x_ref, g_ref, wu_ref, wd_ref, o_ref, h_scr)


def _mlp_final_kernel(x_ref, g_ref, wu_ref, wd_ref, gf_ref, yp_ref, ys_ref, h_scr, *, n_first):
    _mlp_step(x_ref, g_ref, wu_ref, wd_ref, yp_ref, h_scr)

    @pl.when(pl.program_id(1) == pl.num_programs(1) - 1)
    def _():
        y = _rmsnorm(yp_ref[...], gf_ref[...])
        yp_ref[...] = y

        @pl.when(pl.program_id(0) == pl.num_programs(0) - 1)
        def _():
            ys_ref[...] = y[n_first:]


def _mlp(x, norm_g, w_up, w_down, layer, tf, final=None):
    m, d = x.shape
    dff = w_up.shape[2]
    tm = TM_MLP
    in_specs = [pl.BlockSpec((tm, d), lambda i, f: (i, 0), pipeline_mode=SINGLE),
                pl.BlockSpec((None, 1, d), lambda i, f: (layer, 0, 0)),
                pl.BlockSpec((None, d, tf), lambda i, f: (layer, 0, f)),
                pl.BlockSpec((None, tf, d), lambda i, f: (layer, f, 0))]
    common = dict(grid=(m // tm, dff // tf), scratch_shapes=[pltpu.VMEM((tm, d), BF16)],
                  compiler_params=_params(("parallel", "arbitrary")))
    if final is None:
        return pl.pallas_call(
            _mlp_kernel, in_specs=in_specs,
            out_specs=pl.BlockSpec((tm, d), lambda i, f: (i, 0)),
            out_shape=jax.ShapeDtypeStruct((m, d), F32), name="mlp", **common,
        )(x, norm_g, w_up, w_down)
    g_final, mp, ms = final
    kern = functools.partial(_mlp_final_kernel, n_first=tm - ms)
    return pl.pallas_call(
        kern, in_specs=in_specs + [pl.BlockSpec((1, d), lambda i, f: (0, 0))],
        out_specs=[pl.BlockSpec((tm, d), lambda i, f: (i, 0)),
                   pl.BlockSpec((ms, d), lambda i, f: (0, 0))],
        out_shape=[jax.ShapeDtypeStruct((mp, d), F32), jax.ShapeDtypeStruct((ms, d), F32)],
        name="mlp_final", **common,
    )(x, norm_g, w_up, w_down, g_final)


def kernel(x_prompt, x_sample, state_gla, state_ret, state_conv, norm_mix, w_in, w_gate2, b_gate2, gla_norm, ret_norm_w, ret_norm_b, conv_w, w_branch_gla, w_branch_ret, w_branch_conv, w_out, norm_mlp, w_up, w_down, norm_final):
    bp, tp, d = x_prompt.shape
    bs, ts, _ = x_sample.shape
    depth = w_in.shape[0]
    dk, dv = state_gla.shape[3], state_gla.shape[4]
    cdim = state_conv.shape[3]
    mp, ms = bp * tp, bs * ts
    rank_col = 2 * GLA_H * dk + GLA_H * dv
    n_cols = w_in.shape[2] - GLA_RANK
    ret_col0 = rank_col + GLA_H * dv
    conv_col0 = ret_col0 + 3 * RET_H * LANE
    gate_col0 = conv_col0 + 3 * cdim

    r3 = lambda a: a.reshape(depth, 1, a.shape[-1])
    norm_mix3, norm_mlp3, b_gate23 = r3(norm_mix), r3(norm_mlp), r3(b_gate2)
    gla_norm3, ret_w3, ret_b3 = r3(gla_norm), r3(ret_norm_w), r3(ret_norm_b)
    w_gate2_pad = jnp.pad(w_gate2, ((0, 0), (0, LANE - GLA_RANK), (0, 0)))
    w_in_t = jnp.transpose(w_in, (0, 2, 1))

    xs = [x_prompt.reshape(mp, d), x_sample.reshape(ms, d)]
    new_gla_p, new_ret_p, new_conv_p, new_conv_s = [], [], [], []
    new_gla_s = new_ret_s = None
    for l in range(depth):
        h, loga = _prenorm(xs, norm_mix3, w_in_t, w_gate2_pad, b_gate23, l, rank_col)
        z = _inproj(h, w_in_t, l, n_cols, rank_col)

        oa, sg_p = _gla_prompt(z, loga, gla_norm3, l, bp, tp, dk, dv, tb=2 * LANE)
        oa, new_gla_s = _gla_sample(z, loga, gla_norm3, state_gla, oa, new_gla_s, l, mp, bs, ts, dk, dv)
        ob, sr_p = _ret_prompt(z, ret_w3, ret_b3, l, bp, tp, ret_col0, tb=LANE)
        ob, new_ret_s = _ret_sample(z, ret_w3, ret_b3, state_ret, ob, new_ret_s, l, mp, bs, ts, ret_col0)

        oc, sc_p = _conv_prompt(z, conv_w, l, bp, tp, conv_col0, cdim, 2 * LANE)
        sc = state_conv[l]
        pad_rows = jnp.zeros((bs, ts - 1, cdim), F32)
        prev1 = jnp.concatenate([sc[:, 1:2], pad_rows], axis=1).reshape(ms, cdim)
        prev2 = jnp.concatenate([sc, pad_rows[:, 1:]], axis=1).reshape(ms, cdim)
        oc, u_s = _conv_sample(z, conv_w, prev1, prev2, oc, l, mp, ms, ts, conv_col0, cdim, 2 * LANE)

        mm = _merge(oa, ob, oc, z, w_branch_gla, w_branch_ret, w_branch_conv, l, gate_col0, d)
        x = _outproj(mm, w_out, xs, l)
        if l + 1 < depth:
            xs = [_mlp(x, norm_mlp3, w_up, w_down, l, tf=TN)]
        else:
            y_p, y_s = _mlp(x, norm_mlp3, w_up, w_down, l, tf=TN, final=(norm_final.reshape(1, d), mp, ms))

        new_gla_p.append(sg_p)
        new_ret_p.append(sr_p)
        new_conv_p.append(sc_p)
        new_conv_s.append(u_s.reshape(bs, ts, cdim)[:, ts - (CONV_W - 1):])

    return (y_p.reshape(bp, tp, d), y_s.reshape(bs, ts, d),
            jnp.stack(new_gla_p), jnp.stack(new_ret_p), jnp.stack(new_conv_p),
            new_gla_s, new_ret_s, jnp.stack(new_conv_s))
```
